```python
import jax, jax.numpy as jnp
from jax import lax
import numpy as np

D_MODEL = 1024
BATCH = 4
SEQ = 8192
DEPTH = 2

BLOCK = 128
HEAD_DIM = 64
N_HEADS_FOX = 16
N_HEADS_DIL = 16
WIDTH_FOX = N_HEADS_FOX * HEAD_DIM
WIDTH_DIL = N_HEADS_DIL * HEAD_DIM
DIL_PATTERNS = ((128, 1), (512, 4), (2048, 16))
N_HEADS_RET = 4
RET_QK_DIM = 256
RET_V_DIM = 512
RET_CHUNK = 128
ROT_BASE = 10000.0
EPS = 1e-6
NEG = -1e30
N_EVEN = (DEPTH + 1) // 2
N_ODD = DEPTH // 2
EVEN_IN = 4 * WIDTH_FOX + N_HEADS_FOX + 4 * WIDTH_DIL
EVEN_MIX = WIDTH_FOX + WIDTH_DIL
ODD_IN = 2 * N_HEADS_RET * RET_QK_DIM + 2 * N_HEADS_RET * RET_V_DIM
ODD_MIX = N_HEADS_RET * RET_V_DIM

kernel_name = "hybrid_fox_dilated_retention_trunk"

f32 = jnp.float32


def rms_norm(x, g):
    xf = x.astype(f32)
    y = xf * lax.rsqrt(jnp.mean(xf * xf, axis=-1, keepdims=True) + EPS)
    return (y * g.astype(f32)).astype(x.dtype)


def forgetting_attention(q, k, v, log_f):
    Bsz, S, H, hd = q.shape
    scale = hd ** -0.5
    c = jnp.cumsum(log_f, axis=1).transpose(0, 2, 1)
    outs = []
    for i in range(S // BLOCK):
        q0, e = i * BLOCK, (i + 1) * BLOCK
        s = jnp.einsum('bqhd,bkhd->bhqk', q[:, q0:e], k[:, :e]).astype(f32) * scale
        s = s + c[:, :, q0:e, None] - c[:, :, None, :e]
        causal = jnp.arange(e)[None, :] <= (q0 + jnp.arange(BLOCK))[:, None]
        p = jax.nn.softmax(jnp.where(causal, s, NEG), axis=-1)
        outs.append(jnp.einsum('bhqk,bkhd->bqhd', p.astype(v.dtype), v[:, :e]))
    return jnp.concatenate(outs, axis=1)


def dilated_pattern(q, k, v, window, dilation):
    Bsz, S, H, hd = q.shape
    n_keys = window // dilation
    span = dilation * BLOCK
    Sp = -(-S // span) * span
    L = Sp // dilation
    nb = L // BLOCK

    def streams(t):
        t = jnp.pad(t, ((0, 0), (0, Sp - S), (0, 0), (0, 0))).reshape(Bsz, L, dilation, H, hd)
        return t.transpose(0, 2, 3, 1, 4).reshape(Bsz, dilation, H, nb, BLOCK, hd)

    def with_prev(t):
        prev = jnp.pad(t, ((0, 0), (0, 0), (0, 0), (1, 0), (0, 0), (0, 0)))[:, :, :, :-1]
        return jnp.concatenate([prev, t], axis=4)

    qs = streams(q)
    kb, vb = with_prev(streams(k)), with_prev(streams(v))
    s = jnp.einsum('brhnqd,brhnkd->brhnqk', qs, kb).astype(f32) * (hd ** -0.5)
    qi = jnp.arange(BLOCK)[:, None]
    kj = jnp.arange(2 * BLOCK)[None, :]
    dist = BLOCK + qi - kj
    band = (dist >= 0) & (dist <= n_keys)
    has_prev = (jnp.arange(nb) > 0)[:, None, None] | (kj >= BLOCK)[None]
    mask = band[None] & has_prev
    s = jnp.where(mask, s, NEG)
    lse = jax.nn.logsumexp(s, axis=-1)
    p = jnp.exp(s - lse[..., None])
    o = jnp.einsum('brhnqk,brhnkd->brhnqd', p.astype(v.dtype), vb)
    o = o.reshape(Bsz, dilation, H, L, hd).transpose(0, 3, 1, 2, 4).reshape(Bsz, Sp, H, hd)[:, :S]
    lse = lse.reshape(Bsz, dilation, H, L).transpose(0, 3, 1, 2).reshape(Bsz, Sp, H)[:, :S]
    return o, lse


def dilated_attention(q, k, v):
    outs, lses = [], []
    for window, dilation in DIL_PATTERNS:
        o, lse = dilated_pattern(q, k, v, window, dilation)
        outs.append(o)
        lses.append(lse)
    wts = jax.nn.softmax(jnp.stack(lses), axis=0)
    return jnp.einsum('pbsh,pbshd->bshd', wts.astype(q.dtype), jnp.stack(outs))


def rotate(x):
    S, half = x.shape[1], x.shape[-1] // 2
    inv = 1.0 / (ROT_BASE ** jnp.linspace(0.0, 1.0, half, dtype=f32))
    ang = jnp.arange(S, dtype=f32)[:, None] * inv[None, :]
    cos, sin = jnp.cos(ang)[None, :, None, :], jnp.sin(ang)[None, :, None, :]
    x1, x2 = x[..., 0::2].astype(f32), x[..., 1::2].astype(f32)
    y = jnp.stack([x1 * cos - x2 * sin, x1 * sin + x2 * cos], axis=-1)
    return y.reshape(x.shape).astype(x.dtype)


def retention(q, k, v):
    Bsz, S, H, dk = q.shape
    dv = v.shape[-1]
    C = RET_CHUNK
    nC = S // C
    log_gamma = jnp.log1p(-jnp.power(2.0, -5.0 - jnp.arange(H, dtype=f32)))
    pos = jnp.arange(C, dtype=f32)
    rel = pos[:, None] - pos[None, :]
    intra = jnp.where(rel >= 0, jnp.exp(log_gamma[:, None, None] * jnp.maximum(rel, 0.0)), 0.0)
    chunks = lambda t: t.reshape(Bsz, nC, C, H, t.shape[-1]).transpose(1, 0, 3, 2, 4)
    qc, kc, vc = chunks(q), chunks(k), chunks(v)
    s = jnp.einsum('nbhqd,nbhkd->nbhqk', qc, kc).astype(f32) * intra
    inner = jnp.einsum('nbhqk,nbhkd->nbhqd', s.astype(v.dtype), vc).astype(f32)
    q_decay = jnp.exp(log_gamma[:, None] * (pos + 1.0))[None, :, :, None]
    k_decay = jnp.exp(log_gamma[:, None] * (C - 1.0 - pos))[None, :, :, None]
    chunk_decay = jnp.exp(log_gamma * C)[None, :, None, None]

    def step(state, inp):
        q_i, k_i, v_i = inp
        cross = jnp.einsum('bhqd,bhde->bhqe', q_i.astype(f32), state) * q_decay
        state = state * chunk_decay + jnp.einsum('bhkd,bhke->bhde', k_i.astype(f32) * k_decay, v_i.astype(f32))
        return state, cross

    _, cross = lax.scan(step, jnp.zeros((Bsz, H, dk, dv), f32), (qc, kc, vc))
    out = inner + cross
    return out.transpose(1, 0, 3, 2, 4).reshape(Bsz, S, H, dv)


def head_group_norm(y):
    mu = jnp.mean(y, axis=-1, keepdims=True)
    var = jnp.mean(jnp.square(y - mu), axis=-1, keepdims=True)
    return (y - mu) * lax.rsqrt(var + EPS)


def even_layer(x, g_norm, w_in, b_f, w_out):
    Bsz, S, _ = x.shape
    z = rms_norm(x, g_norm) @ w_in
    cuts = [WIDTH_FOX, 2 * WIDTH_FOX, 3 * WIDTH_FOX, 4 * WIDTH_FOX,
            4 * WIDTH_FOX + N_HEADS_FOX,
            4 * WIDTH_FOX + N_HEADS_FOX + WIDTH_DIL,
            4 * WIDTH_FOX + N_HEADS_FOX + 2 * WIDTH_DIL,
            4 * WIDTH_FOX + N_HEADS_FOX + 3 * WIDTH_DIL]
    qa, ka, va, ga, fa, qb, kb, vb, gb = jnp.split(z, cuts, axis=-1)
    hA = lambda t: t.reshape(Bsz, S, N_HEADS_FOX, HEAD_DIM)
    hB = lambda t: t.reshape(Bsz, S, N_HEADS_DIL, HEAD_DIM)
    log_f = jax.nn.log_sigmoid(fa.astype(f32) + b_f.astype(f32))
    ya = forgetting_attention(hA(qa), hA(ka), hA(va), log_f).reshape(Bsz, S, WIDTH_FOX)
    yb = dilated_attention(hB(qb), hB(kb), hB(vb)).reshape(Bsz, S, WIDTH_DIL)
    y = jnp.concatenate([ya * jax.nn.silu(ga), yb * jax.nn.silu(gb)], axis=-1)
    return x + y @ w_out


def odd_layer(x, g_norm, w_in, w_out):
    Bsz, S, _ = x.shape
    qk_w, v_w = N_HEADS_RET * RET_QK_DIM, N_HEADS_RET * RET_V_DIM
    z = rms_norm(x, g_norm) @ w_in
    q, k, v, g = jnp.split(z, [qk_w, 2 * qk_w, 2 * qk_w + v_w], axis=-1)
    q = rotate(q.reshape(Bsz, S, N_HEADS_RET, RET_QK_DIM))
    k = rotate(k.reshape(Bsz, S, N_HEADS_RET, RET_QK_DIM)) * (RET_QK_DIM ** -0.5)
    y = retention(q, k, v.reshape(Bsz, S, N_HEADS_RET, RET_V_DIM))
    y = head_group_norm(y).reshape(Bsz, S, v_w).astype(x.dtype)
    return x + (y * jax.nn.silu(g)) @ w_out


def setup_inputs(seed: int = 0) -> dict:
    key = jax.random.key(seed)
    ks = jax.random.split(key, 12)
    x = jax.random.normal(ks[0], (BATCH, SEQ, D_MODEL), f32)
    even_norm = 1.0 + 0.02 * jax.random.normal(ks[1], (N_EVEN, D_MODEL), f32)
    even_w_in = jax.random.normal(ks[2], (N_EVEN, D_MODEL, EVEN_IN), f32) * D_MODEL ** -0.5
    even_b_f = jax.random.uniform(ks[3], (N_EVEN, N_HEADS_FOX), f32, 1.0, 5.0)
    even_w_out = jax.random.normal(ks[4], (N_EVEN, EVEN_MIX, D_MODEL), f32) * EVEN_MIX ** -0.5
    odd_norm = 1.0 + 0.02 * jax.random.normal(ks[5], (N_ODD, D_MODEL), f32)
    odd_w_in = jax.random.normal(ks[6], (N_ODD, D_MODEL, ODD_IN), f32) * D_MODEL ** -0.5
    odd_w_out = jax.random.normal(ks[7], (N_ODD, ODD_MIX, D_MODEL), f32) * ODD_MIX ** -0.5
    final_norm = 1.0 + 0.02 * jax.random.normal(ks[8], (D_MODEL,), f32)
    return {"x": x, "even_norm": even_norm, "even_w_in": even_w_in, "even_b_f": even_b_f,
            "even_w_out": even_w_out, "odd_norm": odd_norm, "odd_w_in": odd_w_in,
            "odd_w_out": odd_w_out, "final_norm": final_norm}


def reference(x, even_norm, even_w_in, even_b_f, even_w_out, odd_norm, odd_w_in, odd_w_out, final_norm):
    for layer in range(DEPTH):
        j = layer // 2
        if layer % 2 == 0:
            x = even_layer(x, even_norm[j], even_w_in[j], even_b_f[j], even_w_out[j])
        else:
            x = odd_layer(x, odd_norm[j], odd_w_in[j], odd_w_out[j])
    return rms_norm(x, final_norm)
```

```python
import functools

import jax
import jax.numpy as jnp
from jax import lax
from jax.experimental import pallas as pl
from jax.experimental.pallas import tpu as pltpu

f32 = jnp.float32
bf16 = jnp.bfloat16

LANE = 128
HEAD_DIM = 64
N_HEADS_FOX = 16
N_HEADS_DIL = 16
WIDTH_FOX = N_HEADS_FOX * HEAD_DIM
WIDTH_DIL = N_HEADS_DIL * HEAD_DIM
DIL_BLOCK = 128
DIL_PATTERNS = ((128, 1), (512, 4), (2048, 16))
DIL_CHUNK = 2048
N_HEADS_RET = 4
RET_QK_DIM = 256
RET_V_DIM = 512
ROT_BASE = 10000.0
EPS = 1e-6
NEG = -1e30

PAIRS_FOX = WIDTH_FOX // LANE
PAIRS_DIL = WIDTH_DIL // LANE


def _silu(g):
    return g / (1.0 + jnp.exp(-g))


def _in_proj_kernel(*refs, n_groups, with_forget, n_rot_blocks):
    it = iter(refs)
    x_ref, g_ref, w_ref = next(it), next(it), next(it)
    wf_ref = next(it) if with_forget else None
    cos_ref = sin_ref = None
    if n_rot_blocks:
        cos_ref, sin_ref = next(it), next(it)
    o_ref = next(it)
    f_ref = next(it) if with_forget else None
    h_ref = next(it)

    j = pl.program_id(1)

    @pl.when(j == 0)
    def _():
        x = x_ref[...]
        ms = jnp.mean(x * x, axis=-1, keepdims=True)
        h = (x * lax.rsqrt(ms + EPS) * g_ref[...]).astype(bf16)
        h_ref[...] = h
        if with_forget:
            f_ref[...] = jnp.dot(h, wf_ref[...], preferred_element_type=f32)

    acc = jnp.dot(h_ref[...], w_ref[...], preferred_element_type=f32)

    def store_plain():
        for c in range(n_groups):
            o_ref[c] = acc[:, c * LANE:(c + 1) * LANE].astype(bf16)

    if not n_rot_blocks:
        store_plain()
        return

    @pl.when(j < n_rot_blocks)
    def _():
        cos, sin = cos_ref[...], sin_ref[...]
        for c in range(0, n_groups, 2):
            x1 = acc[:, c * LANE:(c + 1) * LANE]
            x2 = acc[:, (c + 1) * LANE:(c + 2) * LANE]
            o_ref[c] = (x1 * cos - x2 * sin).astype(bf16)
            o_ref[c + 1] = (x1 * sin + x2 * cos).astype(bf16)

    @pl.when(j >= n_rot_blocks)
    def _():
        store_plain()


def _in_proj(x2d, gain, w, *, seq, wf=None, rot=None, n_rot_blocks=0, tm=1024, bn=1024, name):
    T, D = x2d.shape
    N = w.shape[1]
    tm = min(tm, seq)
    n_groups = bn // LANE
    grid = (T // tm, N // bn)
    in_specs = [
        pl.BlockSpec((tm, D), lambda i, j: (i, 0)),
        pl.BlockSpec((1, D), lambda i, j: (0, 0)),
        pl.BlockSpec((D, bn), lambda i, j: (0, j)),
    ]
    args = [x2d, gain.reshape(1, D).astype(f32), w]
    out_shape = [jax.ShapeDtypeStruct((N // LANE, T, LANE), bf16)]
    out_specs = [pl.BlockSpec((n_groups, tm, LANE), lambda i, j: (j, i, 0))]
    if wf is not None:
        in_specs.append(pl.BlockSpec((D, LANE), lambda i, j: (0, 0)))
        args.append(wf)
        out_shape.append(jax.ShapeDtypeStruct((T, LANE), f32))
        out_specs.append(pl.BlockSpec((tm, LANE), lambda i, j: (i, 0)))
    if n_rot_blocks:
        spb = seq // tm
        for t in rot:
            in_specs.append(pl.BlockSpec((tm, LANE), lambda i, j: (i % spb, 0)))
            args.append(t)
    kern = functools.partial(_in_proj_kernel, n_groups=n_groups,
                             with_forget=wf is not None, n_rot_blocks=n_rot_blocks)
    outs = pl.pallas_call(
        kern,
        grid=grid,
        in_specs=in_specs,
        out_specs=out_specs,
        out_shape=out_shape,
        scratch_shapes=[pltpu.VMEM((tm, D), bf16)],
        compiler_params=pltpu.CompilerParams(
            dimension_semantics=("arbitrary", "arbitrary")),
        name=name,
    )(*args)
    return outs


def _cumsum_kernel(f_ref, b_ref, o_ref, carry_ref, *, tc):
    @pl.when(pl.program_id(1) == 0)
    def _():
        carry_ref[...] = jnp.zeros_like(carry_ref)

    z = f_ref[...] + b_ref[...]
    lf = jnp.minimum(z, 0.0) - jnp.log1p(jnp.exp(-jnp.abs(z)))
    row = lax.broadcasted_iota(jnp.int32, (tc, tc), 0)
    col = lax.broadcasted_iota(jnp.int32, (tc, tc), 1)
    tri = (col <= row).astype(f32)
    c = jnp.dot(tri, lf, preferred_element_type=f32,
                precision=lax.Precision.HIGHEST) + carry_ref[...]
    o_ref[...] = c
    carry_ref[...] = c[tc - 1:tc, :]


def _forget_cumsum(f2d, b_pad, *, batch, seq, tc=256):
    tc = min(tc, seq)
    nb = seq // tc
    return pl.pallas_call(
        functools.partial(_cumsum_kernel, tc=tc),
        grid=(batch, nb),
        in_specs=[pl.BlockSpec((tc, LANE), lambda b, i: (b * nb + i, 0)),
                  pl.BlockSpec((1, LANE), lambda b, i: (0, 0))],
        out_specs=pl.BlockSpec((tc, LANE), lambda b, i: (b * nb + i, 0)),
        out_shape=jax.ShapeDtypeStruct(f2d.shape, f32),
        scratch_shapes=[pltpu.VMEM((1, LANE), f32)],
        compiler_params=pltpu.CompilerParams(
            dimension_semantics=("arbitrary", "arbitrary")),
        name="forget_cumsum",
    )(f2d, b_pad)


def _fox_kernel(q_ref, k_ref, v_ref, g_ref, ccol_ref, crow_ref, o_ref, *, tq):
    pair = pl.program_id(1)
    qi = pl.program_id(2)
    lane = lax.broadcasted_iota(jnp.int32, (1, LANE), 1)
    lo = lane < HEAD_DIM
    q = q_ref[0]
    ccol = ccol_ref[...]
    hlane = lax.broadcasted_iota(jnp.int32, (tq, LANE), 1)
    rows = lax.broadcasted_iota(jnp.int32, (tq, tq), 0)
    cols = lax.broadcasted_iota(jnp.int32, (tq, tq), 1)
    causal = cols <= rows

    outs = []
    for h in range(2):
        qh = jnp.where(lo if h == 0 else jnp.logical_not(lo), q, jnp.zeros_like(q))
        cq = jnp.sum(jnp.where(hlane == 2 * pair + h, ccol, 0.0), axis=-1, keepdims=True)

        def step(j, carry, masked):
            m, l, acc = carry
            r0 = pl.multiple_of(j * tq, tq)
            k = k_ref[0, pl.ds(r0, tq), :]
            v = v_ref[0, pl.ds(r0, tq), :]
            ck = crow_ref[0, h, pl.ds(j, 1), :]
            s = lax.dot_general(qh, k, (((1,), (1,)), ((), ())),
                                preferred_element_type=f32)
            s = s + (cq - ck)
            if masked:
                s = jnp.where(causal, s, NEG)
            m_new = jnp.maximum(m, jnp.max(s, axis=-1, keepdims=True))
            alpha = jnp.exp(m - m_new)
            p = jnp.exp(s - m_new)
            l = alpha * l + jnp.sum(p, axis=-1, keepdims=True)
            acc = alpha * acc + jnp.dot(p.astype(bf16), v, preferred_element_type=f32)
            return m_new, l, acc

        init = (jnp.full((tq, 1), NEG, f32), jnp.zeros((tq, 1), f32),
                jnp.zeros((tq, LANE), f32))
        carry = lax.fori_loop(0, qi, functools.partial(step, masked=False), init)
        _, l, acc = step(qi, carry, True)
        outs.append(acc / l)

    y = jnp.where(lo, outs[0], outs[1])
    o_ref[0] = (y * _silu(g_ref[0].astype(f32))).astype(bf16)


def _fox_attention(z, c2d, c_row, *, batch, seq, tq=512):
    tq = min(tq, seq)
    nq = seq // tq
    P = PAIRS_FOX
    T = batch * seq
    return pl.pallas_call(
        functools.partial(_fox_kernel, tq=tq),
        grid=(batch, P, nq),
        in_specs=[
            pl.BlockSpec((1, tq, LANE), lambda b, p, i: (p, b * nq + i, 0)),
            pl.BlockSpec((1, seq, LANE), lambda b, p, i: (P + p, b, 0)),
            pl.BlockSpec((1, seq, LANE), lambda b, p, i: (2 * P + p, b, 0)),
            pl.BlockSpec((1, tq, LANE), lambda b, p, i: (3 * P + p, b * nq + i, 0)),
            pl.BlockSpec((tq, LANE), lambda b, p, i: (b * nq + i, 0)),
            pl.BlockSpec((1, 2, nq, tq), lambda b, p, i: (b, p, 0, 0)),
        ],
        out_specs=pl.BlockSpec((1, tq, LANE), lambda b, p, i: (p, b * nq + i, 0)),
        out_shape=jax.ShapeDtypeStruct((P, T, LANE), bf16),
        compiler_params=pltpu.CompilerParams(
            dimension_semantics=("arbitrary", "arbitrary", "arbitrary")),
        name="fox_attention",
    )(z, z, z, z, c2d, c_row)


def _dil_kernel(q_ref, kp_ref, kc_ref, vp_ref, vc_ref, g_ref, o_ref,
                qs_ref, ks_ref, vs_ref, os_ref, ls_ref):
    chunk = pl.program_id(2)
    CH, BLK = DIL_CHUNK, DIL_BLOCK
    qs_ref[...] = q_ref[0].astype(f32)
    ks_ref[0:CH, :] = kp_ref[0].astype(f32)
    ks_ref[CH:2 * CH, :] = kc_ref[0].astype(f32)
    vs_ref[0:CH, :] = vp_ref[0].astype(f32)
    vs_ref[CH:2 * CH, :] = vc_ref[0].astype(f32)

    lane = lax.broadcasted_iota(jnp.int32, (1, LANE), 1)
    lo = lane < HEAD_DIM
    qi = lax.broadcasted_iota(jnp.int32, (BLK, 2 * BLK), 0)
    kj = lax.broadcasted_iota(jnp.int32, (BLK, 2 * BLK), 1)
    dist = BLK + qi - kj

    for p, (window, r) in enumerate(DIL_PATTERNS):
        n_keys = window // r
        band = (dist >= 0) & (dist <= n_keys)
        span = r * BLK

        def tile(t, _, r=r, span=span, band=band, p=p):
            qstart = (t // r) * span + t % r
            kstart = CH + qstart - span
            kpos = chunk * CH + (qstart - span) + r * kj
            mask = band & (kpos >= 0)
            qt = qs_ref[pl.ds(qstart, BLK, stride=r), :].astype(bf16)
            kt = ks_ref[pl.ds(kstart, 2 * BLK, stride=r), :].astype(bf16)
            vt = vs_ref[pl.ds(kstart, 2 * BLK, stride=r), :].astype(bf16)
            o_h, lse_h = [], []
            for h in range(2):
                qh = jnp.where(lo if h == 0 else jnp.logical_not(lo), qt, jnp.zeros_like(qt))
                s = lax.dot_general(qh, kt, (((1,), (1,)), ((), ())),
                                    preferred_element_type=f32)
                s = jnp.where(mask, s, NEG)
                m = jnp.max(s, axis=-1, keepdims=True)
                e = jnp.exp(s - m)
                l = jnp.sum(e, axis=-1, keepdims=True)
                o = jnp.dot(e.astype(bf16), vt, preferred_element_type=f32) / l
                o_h.append(o)
                lse_h.append(m + jnp.log(l))
            os_ref[p, pl.ds(qstart, BLK, stride=r), :] = jnp.where(lo, o_h[0], o_h[1])
            ls_ref[p, pl.ds(qstart, BLK, stride=r), :] = jnp.where(lo, lse_h[0], lse_h[1])
            return 0

        lax.fori_loop(0, CH // BLK, tile, 0)

    l0, l1, l2 = ls_ref[0], ls_ref[1], ls_ref[2]
    mx = jnp.maximum(jnp.maximum(l0, l1), l2)
    w0, w1, w2 = jnp.exp(l0 - mx), jnp.exp(l1 - mx), jnp.exp(l2 - mx)
    y = (w0 * os_ref[0] + w1 * os_ref[1] + w2 * os_ref[2]) / (w0 + w1 + w2)
    o_ref[0] = (y * _silu(g_ref[0].astype(f32))).astype(bf16)


def _dilated_attention(z, *, batch, seq, group0):
    CH = DIL_CHUNK
    assert seq % CH == 0
    nc = seq // CH
    P = PAIRS_DIL
    T = batch * seq
    g0 = group0

    def cur(off):
        return lambda b, p, c: (g0 + off * P + p, b * nc + c, 0)

    def prev(off):
        return lambda b, p, c: (g0 + off * P + p, b * nc + jnp.maximum(c - 1, 0), 0)

    blk = (1, CH, LANE)
    return pl.pallas_call(
        _dil_kernel,
        grid=(batch, P, nc),
        in_specs=[
            pl.BlockSpec(blk, cur(0)),
            pl.BlockSpec(blk, prev(1)), pl.BlockSpec(blk, cur(1)),
            pl.BlockSpec(blk, prev(2)), pl.BlockSpec(blk, cur(2)),
            pl.BlockSpec(blk, cur(3)),
        ],
        out_specs=pl.BlockSpec(blk, lambda b, p, c: (p, b * nc + c, 0)),
        out_shape=jax.ShapeDtypeStruct((P, T, LANE), bf16),
        scratch_shapes=[
            pltpu.VMEM((CH, LANE), f32),
            pltpu.VMEM((2 * CH, LANE), f32),
            pltpu.VMEM((2 * CH, LANE), f32),
            pltpu.VMEM((len(DIL_PATTERNS), CH, LANE), f32),
            pltpu.VMEM((len(DIL_PATTERNS), CH, LANE), f32),
        ],
        compiler_params=pltpu.CompilerParams(
            dimension_semantics=("arbitrary", "arbitrary", "arbitrary")),
        name="dilated_attention",
    )(z, z, z, z, z, z)


def _ret_kernel(q_ref, k_ref, v_ref, g_ref, intra_ref, qd_ref, kd_ref, cd_ref,
                o_ref, state_ref, *, C, n_chunks):
    @pl.when(pl.program_id(2) == 0)
    def _():
        state_ref[...] = jnp.zeros_like(state_ref)

    nq = RET_QK_DIM // LANE
    nv = RET_V_DIM // LANE
    intra = intra_ref[0]
    qd = qd_ref[0]
    kd = kd_ref[0]
    cd = cd_ref[0]

    def chunk(ci, _):
        r0 = pl.multiple_of(ci * C, C)
        q = jnp.concatenate([q_ref[c, pl.ds(r0, C), :] for c in range(nq)], axis=1)
        k = jnp.concatenate([k_ref[c, pl.ds(r0, C), :] for c in range(nq)], axis=1)
        v = jnp.concatenate([v_ref[c, pl.ds(r0, C), :] for c in range(nv)], axis=1)
        s = lax.dot_general(q, k, (((1,), (1,)), ((), ())),
                            preferred_element_type=f32) * intra
        inner = jnp.dot(s.astype(bf16), v, preferred_element_type=f32)
        st = state_ref[...]
        cross = jnp.dot(q, st.astype(bf16), preferred_element_type=f32) * qd
        kdec = (k.astype(f32) * kd).astype(bf16)
        upd = lax.dot_general(kdec, v, (((0,), (0,)), ((), ())),
                              preferred_element_type=f32)
        state_ref[...] = st * cd + upd
        y = inner + cross
        mu = jnp.mean(y, axis=-1, keepdims=True)
        yc = y - mu
        var = jnp.mean(yc * yc, axis=-1, keepdims=True)
        yn = yc * lax.rsqrt(var + EPS)
        g = jnp.concatenate([g_ref[c, pl.ds(r0, C), :] for c in range(nv)],
                            axis=1).astype(f32)
        out = (yn * _silu(g)).astype(bf16)
        for c in range(nv):
            o_ref[c, pl.ds(r0, C), :] = out[:, c * LANE:(c + 1) * LANE]
        return 0

    lax.fori_loop(0, n_chunks, chunk, 0)


def _retention(z, *, batch, seq, C=128, tr=1024):
    tr = min(tr, seq)
    nr = seq // tr
    H = N_HEADS_RET
    T = batch * seq
    nq = RET_QK_DIM // LANE
    nv = RET_V_DIM // LANE
    kq0 = H
    v0 = 2 * H * nq // nv
    g0 = v0 + H

    log_gamma = jnp.log1p(-jnp.power(2.0, -5.0 - jnp.arange(H, dtype=f32)))
    pos = jnp.arange(C, dtype=f32)
    rel = pos[:, None] - pos[None, :]
    intra = jnp.where(rel >= 0,
                      jnp.exp(log_gamma[:, None, None] * jnp.maximum(rel, 0.0)), 0.0)
    q_decay = jnp.exp(log_gamma[:, None] * (pos + 1.0))[:, :, None]
    k_decay = jnp.exp(log_gamma[:, None] * (C - 1.0 - pos))[:, :, None]
    chunk_decay = jnp.exp(log_gamma * C)[:, None, None]

    return pl.pallas_call(
        functools.partial(_ret_kernel, C=C, n_chunks=tr // C),
        grid=(batch, H, nr),
        in_specs=[
            pl.BlockSpec((nq, tr, LANE), lambda b, h, i: (h, b * nr + i, 0)),
            pl.BlockSpec((nq, tr, LANE), lambda b, h, i: (kq0 + h, b * nr + i, 0)),
            pl.BlockSpec((nv, tr, LANE), lambda b, h, i: (v0 + h, b * nr + i, 0)),
            pl.BlockSpec((nv, tr, LANE), lambda b, h, i: (g0 + h, b * nr + i, 0)),
            pl.BlockSpec((1, C, C), lambda b, h, i: (h, 0, 0)),
            pl.BlockSpec((1, C, 1), lambda b, h, i: (h, 0, 0)),
            pl.BlockSpec((1, C, 1), lambda b, h, i: (h, 0, 0)),
            pl.BlockSpec((1, 1, 1), lambda b, h, i: (h, 0, 0)),
        ],
        out_specs=pl.BlockSpec((nv, tr, LANE), lambda b, h, i: (h, b * nr + i, 0)),
        out_shape=jax.ShapeDtypeStruct((H * nv, T, LANE), bf16),
        scratch_shapes=[pltpu.VMEM((RET_QK_DIM, RET_V_DIM), f32)],
        compiler_params=pltpu.CompilerParams(
            dimension_semantics=("arbitrary", "arbitrary", "arbitrary")),
        name="retention",
    )(z, z, z, z, intra, q_decay, k_decay, chunk_decay)


def _out_proj_kernel(*refs, n_y, final_norm):
    y_refs = refs[:n_y]
    w_ref, x_ref = refs[n_y], refs[n_y + 1]
    g_ref = refs[n_y + 2] if final_norm else None
    o_ref = refs[-1]
    y = jnp.concatenate([r[c] for r in y_refs for c in range(r.shape[0])], axis=1)
    out = x_ref[...] + jnp.dot(y, w_ref[...], preferred_element_type=f32)
    if final_norm:
        ms = jnp.mean(out * out, axis=-1, keepdims=True)
        out = out * lax.rsqrt(ms + EPS) * g_ref[...]
    o_ref[...] = out


def _out_proj(ys, w, x2d, *, gain=None, tm=512, name):
    T, D = x2d.shape
    tm = min(tm, T)
    in_specs = [pl.BlockSpec((y.shape[0], tm, LANE), lambda i: (0, i, 0)) for y in ys]
    in_specs += [pl.BlockSpec(w.shape, lambda i: (0, 0)),
                 pl.BlockSpec((tm, D), lambda i: (i, 0))]
    args = list(ys) + [w, x2d]
    if gain is not None:
        in_specs.append(pl.BlockSpec((1, D), lambda i: (0, 0)))
        args.append(gain.reshape(1, D).astype(f32))
    return pl.pallas_call(
        functools.partial(_out_proj_kernel, n_y=len(ys), final_norm=gain is not None),
        grid=(T // tm,),
        in_specs=in_specs,
        out_specs=pl.BlockSpec((tm, D), lambda i: (i, 0)),
        out_shape=jax.ShapeDtypeStruct((T, D), f32),
        compiler_params=pltpu.CompilerParams(dimension_semantics=("arbitrary",)),
        name=name,
    )(*args)


def _even_layer(x2d, g_norm, w_in, b_f, w_out, *, batch, seq):
    D = x2d.shape[1]
    a_w = 4 * WIDTH_FOX
    nf = N_HEADS_FOX
    scale = HEAD_DIM ** -0.5
    col_scale = jnp.ones((a_w + 4 * WIDTH_DIL,), f32)
    col_scale = col_scale.at[:WIDTH_FOX].set(scale).at[a_w:a_w + WIDTH_DIL].set(scale)
    w_main = jnp.concatenate([w_in[:, :a_w], w_in[:, a_w + nf:]], axis=1) * col_scale
    w_f = jnp.pad(w_in[:, a_w:a_w + nf], ((0, 0), (0, LANE - nf)))
    z, f2d = _in_proj(x2d, g_norm, w_main.astype(bf16), seq=seq, wf=w_f.astype(bf16),
                      name="even_in_proj")
    b_pad = jnp.pad(b_f.astype(f32), (0, LANE - nf)).reshape(1, LANE)
    c2d = _forget_cumsum(f2d, b_pad, batch=batch, seq=seq)

    tq = min(512, seq)
    c_row = c2d[:, :nf].reshape(batch, seq, nf).transpose(0, 2, 1)
    c_row = c_row.reshape(batch, nf, seq // tq, tq)
    ya = _fox_attention(z, c2d, c_row, batch=batch, seq=seq, tq=tq)
    yb = _dilated_attention(z, batch=batch, seq=seq, group0=4 * PAIRS_FOX)
    return _out_proj([ya, yb], w_out.astype(bf16), x2d, name="even_out_proj")


def _odd_layer(x2d, g_norm, w_in, w_out, *, batch, seq, final_gain):
    H, dk = N_HEADS_RET, RET_QK_DIM
    qk_w = H * dk
    half = dk // 2
    perm = (jnp.arange(H)[:, None, None] * dk
            + jnp.arange(2)[None, :, None]
            + 2 * jnp.arange(half)[None, None, :]).reshape(-1)
    w_q = w_in[:, :qk_w][:, perm]
    w_k = w_in[:, qk_w:2 * qk_w][:, perm] * (dk ** -0.5)
    w_all = jnp.concatenate([w_q, w_k, w_in[:, 2 * qk_w:]], axis=1).astype(bf16)

    inv = 1.0 / (ROT_BASE ** jnp.linspace(0.0, 1.0, half, dtype=f32))
    ang = jnp.arange(seq, dtype=f32)[:, None] * inv[None, :]
    rot = (jnp.cos(ang), jnp.sin(ang))
    (z,) = _in_proj(x2d, g_norm, w_all, seq=seq, rot=rot, n_rot_blocks=2 * qk_w // 1024,
                    name="odd_in_proj")
    yc = _retention(z, batch=batch, seq=seq)
    return _out_proj([yc], w_out.astype(bf16), x2d, gain=final_gain, name="odd_out_proj")


def kernel(x, even_norm, even_w_in, even_b_f, even_w_out, odd_norm, odd_w_in, odd_w_out, final_norm):
    batch, seq, D = x.shape
    x2d = x.reshape(batch * seq, D)
    x2d = _even_layer(x2d, even_norm[0], even_w_in[0], even_b_f[0], even_w_out[0],
                      batch=batch, seq=seq)
    out = _odd_layer(x2d, odd_norm[0], odd_w_in[0], odd_w_out[0],
                     batch=batch, seq=seq, final_gain=final_norm)
    return out.reshape(batch, seq, D)
```

```python
import functools

import numpy as np
import jax
import jax.numpy as jnp
from jax import lax
from jax.experimental import pallas as pl
from jax.experimental.pallas import tpu as pltpu

f32 = jnp.float32
bf16 = jnp.bfloat16

LANE = 128
HEAD_DIM = 64
N_HEADS_FOX = 16
N_HEADS_DIL = 16
WIDTH_FOX = N_HEADS_FOX * HEAD_DIM
WIDTH_DIL = N_HEADS_DIL * HEAD_DIM
DIL_BLOCK = 128
DIL_PATTERNS = ((128, 1), (512, 4), (2048, 16))
DIL_CHUNK = 2048
DIL_UNROLL = 8
N_HEADS_RET = 4
RET_QK_DIM = 256
RET_V_DIM = 512
ROT_BASE = 10000.0
EPS = 1e-6
NEG = -1e30

LOG2E = 1.4426950408889634
N_PIECES = 3
BIAS_LANES = 8

PAIRS_FOX = WIDTH_FOX // LANE
PAIRS_DIL = WIDTH_DIL // LANE


def _silu(g):
    return g / (1.0 + jnp.exp(-g))


def _in_proj_kernel(*refs, n_groups, with_forget, n_rot_blocks):
    it = iter(refs)
    x_ref, g_ref, w_ref = next(it), next(it), next(it)
    wf_ref = next(it) if with_forget else None
    cos_ref = sin_ref = None
    if n_rot_blocks:
        cos_ref, sin_ref = next(it), next(it)
    o_ref = next(it)
    f_ref = next(it) if with_forget else None
    h_ref = next(it)

    j = pl.program_id(1)

    @pl.when(j == 0)
    def _():
        x = x_ref[...]
        ms = jnp.mean(x * x, axis=-1, keepdims=True)
        h = (x * lax.rsqrt(ms + EPS) * g_ref[...]).astype(bf16)
        h_ref[...] = h
        if with_forget:
            f_ref[...] = jnp.dot(h, wf_ref[...], preferred_element_type=f32)

    acc = jnp.dot(h_ref[...], w_ref[...], preferred_element_type=f32)

    def store_plain():
        for c in range(n_groups):
            o_ref[c] = acc[:, c * LANE:(c + 1) * LANE].astype(bf16)

    if not n_rot_blocks:
        store_plain()
        return

    @pl.when(j < n_rot_blocks)
    def _():
        cos, sin = cos_ref[...], sin_ref[...]
        for c in range(0, n_groups, 2):
            x1 = acc[:, c * LANE:(c + 1) * LANE]
            x2 = acc[:, (c + 1) * LANE:(c + 2) * LANE]
            o_ref[c] = (x1 * cos - x2 * sin).astype(bf16)
            o_ref[c + 1] = (x1 * sin + x2 * cos).astype(bf16)

    @pl.when(j >= n_rot_blocks)
    def _():
        store_plain()


def _in_proj(x2d, gain, w, *, seq, wf=None, rot=None, n_rot_blocks=0, tm=1024, bn=1024, name):
    T, D = x2d.shape
    N = w.shape[1]
    tm = min(tm, seq)
    n_groups = bn // LANE
    grid = (T // tm, N // bn)
    in_specs = [
        pl.BlockSpec((tm, D), lambda i, j: (i, 0)),
        pl.BlockSpec((1, D), lambda i, j: (0, 0)),
        pl.BlockSpec((D, bn), lambda i, j: (0, j)),
    ]
    args = [x2d, gain.reshape(1, D).astype(f32), w]
    out_shape = [jax.ShapeDtypeStruct((N // LANE, T, LANE), bf16)]
    out_specs = [pl.BlockSpec((n_groups, tm, LANE), lambda i, j: (j, i, 0))]
    if wf is not None:
        in_specs.append(pl.BlockSpec((D, LANE), lambda i, j: (0, 0)))
        args.append(wf)
        out_shape.append(jax.ShapeDtypeStruct((T, LANE), f32))
        out_specs.append(pl.BlockSpec((tm, LANE), lambda i, j: (i, 0)))
    if n_rot_blocks:
        spb = seq // tm
        for t in rot:
            in_specs.append(pl.BlockSpec((tm, LANE), lambda i, j: (i % spb, 0)))
            args.append(t)
    kern = functools.partial(_in_proj_kernel, n_groups=n_groups,
                             with_forget=wf is not None, n_rot_blocks=n_rot_blocks)
    outs = pl.pallas_call(
        kern,
        grid=grid,
        in_specs=in_specs,
        out_specs=out_specs,
        out_shape=out_shape,
        scratch_shapes=[pltpu.VMEM((tm, D), bf16)],
        compiler_params=pltpu.CompilerParams(
            dimension_semantics=("arbitrary", "arbitrary")),
        name=name,
    )(*args)
    return outs


def _cumsum_kernel(f_ref, b_ref, pq_ref, oq_ref, pk_ref, ok_ref, qb_ref, kb_ref,
                   carry_ref, *, tc):
    @pl.when(pl.program_id(1) == 0)
    def _():
        carry_ref[...] = jnp.zeros_like(carry_ref)

    z = f_ref[...] + b_ref[...]
    lf = jnp.minimum(z, 0.0) - jnp.log1p(jnp.exp(-jnp.abs(z)))
    row = lax.broadcasted_iota(jnp.int32, (tc, tc), 0)
    col = lax.broadcasted_iota(jnp.int32, (tc, tc), 1)
    tri = (col <= row).astype(f32)
    c = jnp.dot(tri, lf, preferred_element_type=f32,
                precision=lax.Precision.HIGHEST) + carry_ref[...]
    carry_ref[...] = c[tc - 1:tc, :]

    c2 = c * LOG2E
    p0 = c2.astype(bf16)
    r1 = c2 - p0.astype(f32)
    p1 = r1.astype(bf16)
    p2 = (r1 - p1.astype(f32)).astype(bf16)
    pieces = jnp.concatenate([p0, p1, p2], axis=1)
    qb = jnp.dot(pieces, pq_ref[...], preferred_element_type=f32) + oq_ref[...]
    kb = jnp.dot(pieces, pk_ref[...], preferred_element_type=f32) + ok_ref[...]
    for g in range(PAIRS_FOX):
        qb_ref[g] = qb[:, g * LANE:(g + 1) * LANE].astype(bf16)
        kb_ref[g] = kb[:, g * LANE:(g + 1) * LANE].astype(bf16)


def _bias_placement():
    P = PAIRS_FOX
    pq = np.zeros((N_PIECES * LANE, P * LANE), np.float32)
    pk = np.zeros((N_PIECES * LANE, P * LANE), np.float32)
    oq = np.zeros((1, P * LANE), np.float32)
    ok = np.zeros((1, P * LANE), np.float32)
    for pair in range(P):
        for h in range(2):
            head = 2 * pair + h
            base = pair * LANE + BIAS_LANES * h
            for piece in range(N_PIECES):
                pq[piece * LANE + head, base + piece] = 1.0
                ok[0, base + piece] = 1.0
                oq[0, base + N_PIECES + piece] = 1.0
                pk[piece * LANE + head, base + N_PIECES + piece] = -1.0
    return (jnp.asarray(pq, bf16), jnp.asarray(oq, f32),
            jnp.asarray(pk, bf16), jnp.asarray(ok, f32))


def _forget_cumsum(f2d, b_pad, *, batch, seq, tc=256):
    tc = min(tc, seq)
    nb = seq // tc
    T = batch * seq
    P = PAIRS_FOX
    pq, oq, pk, ok = _bias_placement()
    full = lambda a: pl.BlockSpec(a.shape, lambda b, i: (0, 0))
    out_spec = pl.BlockSpec((P, tc, LANE), lambda b, i: (0, b * nb + i, 0))
    return pl.pallas_call(
        functools.partial(_cumsum_kernel, tc=tc),
        grid=(batch, nb),
        in_specs=[pl.BlockSpec((tc, LANE), lambda b, i: (b * nb + i, 0)),
                  full(b_pad), full(pq), full(oq), full(pk), full(ok)],
        out_specs=[out_spec, out_spec],
        out_shape=[jax.ShapeDtypeStruct((P, T, LANE), bf16)] * 2,
        scratch_shapes=[pltpu.VMEM((1, LANE), f32)],
        compiler_params=pltpu.CompilerParams(
            dimension_semantics=("arbitrary", "arbitrary")),
        name="forget_cumsum",
    )(f2d, b_pad, pq, oq, pk, ok)


def _fox_kernel(q_ref, k_ref, v_ref, g_ref, qb_ref, kb_ref, o_ref, *, tq):
    qi = pl.program_id(2)
    lane = lax.broadcasted_iota(jnp.int32, (1, LANE), 1)
    lo = lane < HEAD_DIM
    own = (lo, jnp.logical_not(lo))
    own_bias = (lane < BIAS_LANES, (lane >= BIAS_LANES) & (lane < 2 * BIAS_LANES))
    q, qb = q_ref[0], qb_ref[0]
    zero = jnp.zeros_like(q)
    q2 = [jnp.concatenate([jnp.where(own[h], q, zero), jnp.where(own_bias[h], qb, zero)],
                          axis=1) for h in range(2)]
    rows = lax.broadcasted_iota(jnp.int32, (tq, tq), 0)
    cols = lax.broadcasted_iota(jnp.int32, (tq, tq), 1)
    causal = cols <= rows

    def step(j, carry, masked):
        r0 = pl.multiple_of(j * tq, tq)
        k2 = jnp.concatenate([k_ref[0, pl.ds(r0, tq), :], kb_ref[0, pl.ds(r0, tq), :]],
                             axis=1)
        v = v_ref[0, pl.ds(r0, tq), :]
        new = []
        for h in range(2):
            m, acc = carry[h]
            vh = jnp.where(own[h], v, jnp.ones_like(v))
            s = lax.dot_general(q2[h], k2, (((1,), (1,)), ((), ())),
                                preferred_element_type=f32)
            if masked:
                s = jnp.where(causal, s, NEG)
            m_new = jnp.maximum(m, jnp.max(s, axis=-1, keepdims=True))
            alpha = jnp.exp2(m - m_new)
            p = jnp.exp2(s - m_new).astype(bf16)
            acc = alpha * acc + jnp.dot(p, vh, preferred_element_type=f32)
            new.append((m_new, acc))
        return tuple(new)

    init = tuple((jnp.full((tq, 1), NEG, f32), jnp.zeros((tq, LANE), f32))
                 for _ in range(2))
    def step_pair(jj, carry):
        return step(2 * jj + 1, step(2 * jj, carry, False), False)

    carry = lax.fori_loop(0, qi // 2, step_pair, init)
    carry = lax.fori_loop(0, qi % 2, lambda _, c: step(qi - 1, c, False), carry)
    (_, acc0), (_, acc1) = step(qi, carry, True)
    num = jnp.where(lo, acc0, acc1)
    den = pltpu.roll(jnp.where(lo, acc1, acc0), HEAD_DIM, axis=1)
    o_ref[0] = (num / den * _silu(g_ref[0].astype(f32))).astype(bf16)


def _fox_attention(z, qb, kb, *, batch, seq, tq=512):
    tq = min(tq, seq)
    nq = seq // tq
    P = PAIRS_FOX
    T = batch * seq
    q_blk = lambda g0: pl.BlockSpec((1, tq, LANE), lambda b, p, i: (g0 + p, b * nq + i, 0))
    kv_blk = lambda g0: pl.BlockSpec((1, seq, LANE), lambda b, p, i: (g0 + p, b, 0))
    return pl.pallas_call(
        functools.partial(_fox_kernel, tq=tq),
        grid=(batch, P, nq),
        in_specs=[q_blk(0), kv_blk(P), kv_blk(2 * P), q_blk(3 * P), q_blk(0), kv_blk(0)],
        out_specs=q_blk(0),
        out_shape=jax.ShapeDtypeStruct((P, T, LANE), bf16),
        compiler_params=pltpu.CompilerParams(
            dimension_semantics=("arbitrary", "arbitrary", "arbitrary")),
        name="fox_attention",
    )(z, z, z, z, qb, kb)


def _dil_kernel(q_ref, kp_ref, kc_ref, vp_ref, vc_ref, g_ref, o_ref,
                qs_ref, ks_ref, vs_ref, os_ref, ls_ref):
    chunk = pl.program_id(2)
    CH, BLK = DIL_CHUNK, DIL_BLOCK
    qs_ref[...] = q_ref[0].astype(f32)
    ks_ref[0:CH, :] = kp_ref[0].astype(f32)
    ks_ref[CH:2 * CH, :] = kc_ref[0].astype(f32)
    vs_ref[0:CH, :] = vp_ref[0].astype(f32)
    vs_ref[CH:2 * CH, :] = vc_ref[0].astype(f32)

    lane = lax.broadcasted_iota(jnp.int32, (1, LANE), 1)
    lo = lane < HEAD_DIM
    hi = jnp.logical_not(lo)
    qi = lax.broadcasted_iota(jnp.int32, (BLK, 2 * BLK), 0)
    kj = lax.broadcasted_iota(jnp.int32, (BLK, 2 * BLK), 1)
    dist = BLK + qi - kj
    n_tiles = CH // BLK

    for p, (window, r) in enumerate(DIL_PATTERNS):
        n_keys = window // r
        band = (dist >= 0) & (dist <= n_keys)
        bias_band = jnp.where(band, 0.0, NEG)
        bias_first = jnp.where(band & (kj >= BLK), 0.0, NEG)
        span = r * BLK

        def tile(t, r=r, span=span, p=p, bias_band=bias_band, bias_first=bias_first):
            sidx = t // r
            qstart = sidx * span + t % r
            kstart = CH + qstart - span
            no_prev = jnp.logical_and(chunk == 0, sidx == 0)
            bias = jnp.where(no_prev, bias_first, bias_band)
            qt = qs_ref[pl.ds(qstart, BLK, stride=r), :].astype(bf16)
            kt = ks_ref[pl.ds(kstart, 2 * BLK, stride=r), :].astype(bf16)
            vt = vs_ref[pl.ds(kstart, 2 * BLK, stride=r), :].astype(bf16)
            res, ms = [], []
            for own in (lo, hi):
                qh = jnp.where(own, qt, jnp.zeros_like(qt))
                vh = jnp.where(own, vt, jnp.ones_like(vt))
                s = lax.dot_general(qh, kt, (((1,), (1,)), ((), ())),
                                    preferred_element_type=f32) + bias
                m = jnp.max(s, axis=-1, keepdims=True)
                e = jnp.exp(s - m).astype(bf16)
                res.append(jnp.dot(e, vh, preferred_element_type=f32))
                ms.append(m)
            num = jnp.where(lo, res[0], res[1])
            den = pltpu.roll(jnp.where(lo, res[1], res[0]), HEAD_DIM, axis=1)
            os_ref[p, pl.ds(qstart, BLK, stride=r), :] = num / den
            ls_ref[p, pl.ds(qstart, BLK, stride=r), :] = (
                jnp.where(lo, ms[0], ms[1]) + jnp.log(den))

        def group(g, _, tile=tile):
            for u in range(DIL_UNROLL):
                tile(g * DIL_UNROLL + u)
            return 0

        lax.fori_loop(0, n_tiles // DIL_UNROLL, group, 0)

    l0, l1, l2 = ls_ref[0], ls_ref[1], ls_ref[2]
    mx = jnp.maximum(jnp.maximum(l0, l1), l2)
    w0, w1, w2 = jnp.exp(l0 - mx), jnp.exp(l1 - mx), jnp.exp(l2 - mx)
    y = (w0 * os_ref[0] + w1 * os_ref[1] + w2 * os_ref[2]) / (w0 + w1 + w2)
    o_ref[0] = (y * _silu(g_ref[0].astype(f32))).astype(bf16)


def _dilated_attention(z, *, batch, seq, group0):
    CH = DIL_CHUNK
    assert seq % CH == 0
    nc = seq // CH
    P = PAIRS_DIL
    T = batch * seq
    g0 = group0

    def cur(off):
        return lambda b, p, c: (g0 + off * P + p, b * nc + c, 0)

    def prev(off):
        return lambda b, p, c: (g0 + off * P + p, b * nc + jnp.maximum(c - 1, 0), 0)

    blk = (1, CH, LANE)
    return pl.pallas_call(
        _dil_kernel,
        grid=(batch, P, nc),
        in_specs=[
            pl.BlockSpec(blk, cur(0)),
            pl.BlockSpec(blk, prev(1)), pl.BlockSpec(blk, cur(1)),
            pl.BlockSpec(blk, prev(2)), pl.BlockSpec(blk, cur(2)),
            pl.BlockSpec(blk, cur(3)),
        ],
        out_specs=pl.BlockSpec(blk, lambda b, p, c: (p, b * nc + c, 0)),
        out_shape=jax.ShapeDtypeStruct((P, T, LANE), bf16),
        scratch_shapes=[
            pltpu.VMEM((CH, LANE), f32),
            pltpu.VMEM((2 * CH, LANE), f32),
            pltpu.VMEM((2 * CH, LANE), f32),
            pltpu.VMEM((len(DIL_PATTERNS), CH, LANE), f32),
            pltpu.VMEM((len(DIL_PATTERNS), CH, LANE), f32),
        ],
        compiler_params=pltpu.CompilerParams(
            dimension_semantics=("arbitrary", "arbitrary", "arbitrary")),
        name="dilated_attention",
    )(z, z, z, z, z, z)


def _ret_kernel(q_ref, k_ref, v_ref, g_ref, intra_ref, qd_ref, kd_ref, cd_ref,
                o_ref, state_ref, *, C, n_chunks):
    @pl.when(pl.program_id(2) == 0)
    def _():
        state_ref[...] = jnp.zeros_like(state_ref)

    nq = RET_QK_DIM // LANE
    nv = RET_V_DIM // LANE
    intra = intra_ref[0]
    qd = qd_ref[0]
    kd = kd_ref[0]
    cd = cd_ref[0]

    def chunk(ci, _):
        r0 = pl.multiple_of(ci * C, C)
        q = jnp.concatenate([q_ref[c, pl.ds(r0, C), :] for c in range(nq)], axis=1)
        k = jnp.concatenate([k_ref[c, pl.ds(r0, C), :] for c in range(nq)], axis=1)
        v = jnp.concatenate([v_ref[c, pl.ds(r0, C), :] for c in range(nv)], axis=1)
        s = lax.dot_general(q, k, (((1,), (1,)), ((), ())),
                            preferred_element_type=f32) * intra
        inner = jnp.dot(s.astype(bf16), v, preferred_element_type=f32)
        st = state_ref[...]
        cross = jnp.dot(q, st.astype(bf16), preferred_element_type=f32) * qd
        kdec = (k.astype(f32) * kd).astype(bf16)
        upd = lax.dot_general(kdec, v, (((0,), (0,)), ((), ())),
                              preferred_element_type=f32)
        state_ref[...] = st * cd + upd
        y = inner + cross
        mu = jnp.mean(y, axis=-1, keepdims=True)
        yc = y - mu
        var = jnp.mean(yc * yc, axis=-1, keepdims=True)
        yn = yc * lax.rsqrt(var + EPS)
        g = jnp.concatenate([g_ref[c, pl.ds(r0, C), :] for c in range(nv)],
                            axis=1).astype(f32)
        out = (yn * _silu(g)).astype(bf16)
        for c in range(nv):
            o_ref[c, pl.ds(r0, C), :] = out[:, c * LANE:(c + 1) * LANE]
        return 0

    lax.fori_loop(0, n_chunks, chunk, 0)


def _retention(z, *, batch, seq, C=128, tr=1024):
    tr = min(tr, seq)
    nr = seq // tr
    H = N_HEADS_RET
    T = batch * seq
    nq = RET_QK_DIM // LANE
    nv = RET_V_DIM // LANE
    kq0 = H
    v0 = 2 * H * nq // nv
    g0 = v0 + H

    log_gamma = jnp.log1p(-jnp.power(2.0, -5.0 - jnp.arange(H, dtype=f32)))
    pos = jnp.arange(C, dtype=f32)
    rel = pos[:, None] - pos[None, :]
    intra = jnp.where(rel >= 0,
                      jnp.exp(log_gamma[:, None, None] * jnp.maximum(rel, 0.0)), 0.0)
    q_decay = jnp.exp(log_gamma[:, None] * (pos + 1.0))[:, :, None]
    k_decay = jnp.exp(log_gamma[:, None] * (C - 1.0 - pos))[:, :, None]
    chunk_decay = jnp.exp(log_gamma * C)[:, None, None]

    return pl.pallas_call(
        functools.partial(_ret_kernel, C=C, n_chunks=tr // C),
        grid=(batch, H, nr),
        in_specs=[
            pl.BlockSpec((nq, tr, LANE), lambda b, h, i: (h, b * nr + i, 0)),
            pl.BlockSpec((nq, tr, LANE), lambda b, h, i: (kq0 + h, b * nr + i, 0)),
            pl.BlockSpec((nv, tr, LANE), lambda b, h, i: (v0 + h, b * nr + i, 0)),
            pl.BlockSpec((nv, tr, LANE), lambda b, h, i: (g0 + h, b * nr + i, 0)),
            pl.BlockSpec((1, C, C), lambda b, h, i: (h, 0, 0)),
            pl.BlockSpec((1, C, 1), lambda b, h, i: (h, 0, 0)),
            pl.BlockSpec((1, C, 1), lambda b, h, i: (h, 0, 0)),
            pl.BlockSpec((1, 1, 1), lambda b, h, i: (h, 0, 0)),
        ],
        out_specs=pl.BlockSpec((nv, tr, LANE), lambda b, h, i: (h, b * nr + i, 0)),
        out_shape=jax.ShapeDtypeStruct((H * nv, T, LANE), bf16),
        scratch_shapes=[pltpu.VMEM((RET_QK_DIM, RET_V_DIM), f32)],
        compiler_params=pltpu.CompilerParams(
            dimension_semantics=("arbitrary", "arbitrary", "arbitrary")),
        name="retention",
    )(z, z, z, z, intra, q_decay, k_decay, chunk_decay)


def _out_proj_kernel(*refs, n_y, final_norm):
    y_refs = refs[:n_y]
    w_ref, x_ref = refs[n_y], refs[n_y + 1]
    g_ref = refs[n_y + 2] if final_norm else None
    o_ref = refs[-1]
    y = jnp.concatenate([r[c] for r in y_refs for c in range(r.shape[0])], axis=1)
    out = x_ref[...] + jnp.dot(y, w_ref[...], preferred_element_type=f32)
    if final_norm:
        ms = jnp.mean(out * out, axis=-1, keepdims=True)
        out = out * lax.rsqrt(ms + EPS) * g_ref[...]
    o_ref[...] = out


def _out_proj(ys, w, x2d, *, gain=None, tm=512, name):
    T, D = x2d.shape
    tm = min(tm, T)
    in_specs = [pl.BlockSpec((y.shape[0], tm, LANE), lambda i: (0, i, 0)) for y in ys]
    in_specs += [pl.BlockSpec(w.shape, lambda i: (0, 0)),
                 pl.BlockSpec((tm, D), lambda i: (i, 0))]
    args = list(ys) + [w, x2d]
    if gain is not None:
        in_specs.append(pl.BlockSpec((1, D), lambda i: (0, 0)))
        args.append(gain.reshape(1, D).astype(f32))
    return pl.pallas_call(
        functools.partial(_out_proj_kernel, n_y=len(ys), final_norm=gain is not None),
        grid=(T // tm,),
        in_specs=in_specs,
        out_specs=pl.BlockSpec((tm, D), lambda i: (i, 0)),
        out_shape=jax.ShapeDtypeStruct((T, D), f32),
        compiler_params=pltpu.CompilerParams(dimension_semantics=("arbitrary",)),
        name=name,
    )(*args)


def _even_layer(x2d, g_norm, w_in, b_f, w_out, *, batch, seq):
    D = x2d.shape[1]
    a_w = 4 * WIDTH_FOX
    nf = N_HEADS_FOX
    scale = HEAD_DIM ** -0.5
    col_scale = jnp.ones((a_w + 4 * WIDTH_DIL,), f32)
    col_scale = col_scale.at[:WIDTH_FOX].set(scale * LOG2E).at[a_w:a_w + WIDTH_DIL].set(scale)
    w_main = jnp.concatenate([w_in[:, :a_w], w_in[:, a_w + nf:]], axis=1) * col_scale
    w_f = jnp.pad(w_in[:, a_w:a_w + nf], ((0, 0), (0, LANE - nf)))
    z, f2d = _in_proj(x2d, g_norm, w_main.astype(bf16), seq=seq, wf=w_f.astype(bf16),
                      name="even_in_proj")
    b_pad = jnp.pad(b_f.astype(f32), (0, LANE - nf)).reshape(1, LANE)
    qb, kb = _forget_cumsum(f2d, b_pad, batch=batch, seq=seq)
    ya = _fox_attention(z, qb, kb, batch=batch, seq=seq)
    yb = _dilated_attention(z, batch=batch, seq=seq, group0=4 * PAIRS_FOX)
    return _out_proj([ya, yb], w_out.astype(bf16), x2d, name="even_out_proj")


def _odd_layer(x2d, g_norm, w_in, w_out, *, batch, seq, final_gain):
    H, dk = N_HEADS_RET, RET_QK_DIM
    qk_w = H * dk
    half = dk // 2
    perm = (jnp.arange(H)[:, None, None] * dk
            + jnp.arange(2)[None, :, None]
            + 2 * jnp.arange(half)[None, None, :]).reshape(-1)
    w_q = w_in[:, :qk_w][:, perm]
    w_k = w_in[:, qk_w:2 * qk_w][:, perm] * (dk ** -0.5)
    w_all = jnp.concatenate([w_q, w_k, w_in[:, 2 * qk_w:]], axis=1).astype(bf16)

    inv = 1.0 / (ROT_BASE ** jnp.linspace(0.0, 1.0, half, dtype=f32))
    ang = jnp.arange(seq, dtype=f32)[:, None] * inv[None, :]
    rot = (jnp.cos(ang), jnp.sin(ang))
    (z,) = _in_proj(x2d, g_norm, w_all, seq=seq, rot=rot, n_rot_blocks=2 * qk_w // 1024,
                    name="odd_in_proj")
    yc = _retention(z, batch=batch, seq=seq)
    return _out_proj([yc], w_out.astype(bf16), x2d, gain=final_gain, name="odd_out_proj")


def kernel(x, even_norm, even_w_in, even_b_f, even_w_out, odd_norm, odd_w_in, odd_w_out, final_norm):
    batch, seq, D = x.shape
    x2d = x.reshape(batch * seq, D)
    x2d = _even_layer(x2d, even_norm[0], even_w_in[0], even_b_f[0], even_w_out[0],
                      batch=batch, seq=seq)
    out = _odd_layer(x2d, odd_norm[0], odd_w_in[0], odd_w_out[0],
                     batch=batch, seq=seq, final_gain=final_norm)
    return out.reshape(batch, seq, D)
```

```python
import functools

import numpy as np
import jax
import jax.numpy as jnp
from jax import lax
from jax.experimental import pallas as pl
from jax.experimental.pallas import tpu as pltpu

f32 = jnp.float32
bf16 = jnp.bfloat16

LANE = 128
HEAD_DIM = 64
N_HEADS_FOX = 16
N_HEADS_DIL = 16
WIDTH_FOX = N_HEADS_FOX * HEAD_DIM
WIDTH_DIL = N_HEADS_DIL * HEAD_DIM
DIL_BLOCK = 128
DIL_PATTERNS = ((128, 1), (512, 4), (2048, 16))
DIL_CHUNK = 2048
DIL_DEINTERLEAVE = 4
DIL_UNROLL = 8
N_HEADS_RET = 4
RET_QK_DIM = 256
RET_V_DIM = 512
RET_CHUNK = 256
ROT_BASE = 10000.0
EPS = 1e-6
NEG = -1e30

LOG2E = 1.4426950408889634
N_PIECES = 3
BIAS_LANES = 8
FOX_ROW_BLOCK = 32

PAIRS_FOX = WIDTH_FOX // LANE
PAIRS_DIL = WIDTH_DIL // LANE


def _silu(g):
    return g / (1.0 + jnp.exp(-g))


def _in_proj_kernel(*refs, n_groups, with_forget, n_rot_blocks):
    it = iter(refs)
    x_ref, g_ref, w_ref = next(it), next(it), next(it)
    wf_ref = next(it) if with_forget else None
    cos_ref = sin_ref = None
    if n_rot_blocks:
        cos_ref, sin_ref = next(it), next(it)
    o_ref = next(it)
    f_ref = next(it) if with_forget else None
    h_ref = next(it)

    j = pl.program_id(1)

    @pl.when(j == 0)
    def _():
        x = x_ref[...]
        ms = jnp.mean(x * x, axis=-1, keepdims=True)
        h = (x * lax.rsqrt(ms + EPS) * g_ref[...]).astype(bf16)
        h_ref[...] = h
        if with_forget:
            f_ref[...] = jnp.dot(h, wf_ref[...], preferred_element_type=f32)

    acc = jnp.dot(h_ref[...], w_ref[...], preferred_element_type=f32)

    def store_plain():
        for c in range(n_groups):
            o_ref[c] = acc[:, c * LANE:(c + 1) * LANE].astype(bf16)

    if not n_rot_blocks:
        store_plain()
        return

    @pl.when(j < n_rot_blocks)
    def _():
        cos, sin = cos_ref[...], sin_ref[...]
        for c in range(0, n_groups, 2):
            x1 = acc[:, c * LANE:(c + 1) * LANE]
            x2 = acc[:, (c + 1) * LANE:(c + 2) * LANE]
            o_ref[c] = (x1 * cos - x2 * sin).astype(bf16)
            o_ref[c + 1] = (x1 * sin + x2 * cos).astype(bf16)

    @pl.when(j >= n_rot_blocks)
    def _():
        store_plain()


def _in_proj(x2d, gain, w, *, seq, wf=None, rot=None, n_rot_blocks=0, tm=1024, bn=1024, name):
    T, D = x2d.shape
    N = w.shape[1]
    tm = min(tm, seq)
    n_groups = bn // LANE
    grid = (T // tm, N // bn)
    in_specs = [
        pl.BlockSpec((tm, D), lambda i, j: (i, 0)),
        pl.BlockSpec((1, D), lambda i, j: (0, 0)),
        pl.BlockSpec((D, bn), lambda i, j: (0, j)),
    ]
    args = [x2d, gain.reshape(1, D).astype(f32), w]
    out_shape = [jax.ShapeDtypeStruct((N // LANE, T, LANE), bf16)]
    out_specs = [pl.BlockSpec((n_groups, tm, LANE), lambda i, j: (j, i, 0))]
    if wf is not None:
        in_specs.append(pl.BlockSpec((D, LANE), lambda i, j: (0, 0)))
        args.append(wf)
        out_shape.append(jax.ShapeDtypeStruct((T, LANE), f32))
        out_specs.append(pl.BlockSpec((tm, LANE), lambda i, j: (i, 0)))
    if n_rot_blocks:
        spb = seq // tm
        for t in rot:
            in_specs.append(pl.BlockSpec((tm, LANE), lambda i, j: (i % spb, 0)))
            args.append(t)
    kern = functools.partial(_in_proj_kernel, n_groups=n_groups,
                             with_forget=wf is not None, n_rot_blocks=n_rot_blocks)
    outs = pl.pallas_call(
        kern,
        grid=grid,
        in_specs=in_specs,
        out_specs=out_specs,
        out_shape=out_shape,
        scratch_shapes=[pltpu.VMEM((tm, D), bf16)],
        compiler_params=pltpu.CompilerParams(
            dimension_semantics=("arbitrary", "arbitrary")),
        name=name,
    )(*args)
    return outs


def _split_bf16(x):
    p0 = x.astype(bf16)
    r1 = x - p0.astype(f32)
    p1 = r1.astype(bf16)
    p2 = (r1 - p1.astype(f32)).astype(bf16)
    return jnp.concatenate([p0, p1, p2], axis=1)


def _cumsum_kernel(f_ref, b_ref, pq_ref, oq_ref, pk_ref, ok_ref, qb_ref, kb_ref,
                   carry_ref, *, tc):
    @pl.when(pl.program_id(1) == 0)
    def _():
        carry_ref[...] = jnp.zeros_like(carry_ref)

    z = f_ref[...] + b_ref[...]
    lf = jnp.minimum(z, 0.0) - jnp.log1p(jnp.exp(-jnp.abs(z)))
    row = lax.broadcasted_iota(jnp.int32, (tc, tc), 0)
    col = lax.broadcasted_iota(jnp.int32, (tc, tc), 1)
    tri = jnp.where(col <= row, 1.0, 0.0).astype(bf16)
    cs = jnp.dot(tri, _split_bf16(lf), preferred_element_type=f32)
    c = (cs[:, :LANE] + cs[:, LANE:2 * LANE] + cs[:, 2 * LANE:]) + carry_ref[...]
    carry_ref[...] = c[tc - 1:tc, :]

    pieces = _split_bf16(c * LOG2E)
    qb = jnp.dot(pieces, pq_ref[...], preferred_element_type=f32) + oq_ref[...]
    kb = jnp.dot(pieces, pk_ref[...], preferred_element_type=f32) + ok_ref[...]
    qb_ref[...] = qb.astype(bf16)
    kb_ref[...] = kb.astype(bf16)


def _bias_placement():
    assert N_HEADS_FOX * BIAS_LANES == LANE and 2 * N_PIECES <= BIAS_LANES
    pq = np.zeros((N_PIECES * LANE, LANE), np.float32)
    pk = np.zeros((N_PIECES * LANE, LANE), np.float32)
    oq = np.zeros((1, LANE), np.float32)
    ok = np.zeros((1, LANE), np.float32)
    for head in range(N_HEADS_FOX):
        base = BIAS_LANES * head
        for piece in range(N_PIECES):
            pq[piece * LANE + head, base + piece] = 1.0
            ok[0, base + piece] = 1.0
            oq[0, base + N_PIECES + piece] = 1.0
            pk[piece * LANE + head, base + N_PIECES + piece] = -1.0
    return (jnp.asarray(pq, bf16), jnp.asarray(oq, f32),
            jnp.asarray(pk, bf16), jnp.asarray(ok, f32))


def _forget_cumsum(f2d, b_pad, *, batch, seq, tc=512):
    tc = min(tc, seq)
    nb = seq // tc
    T = batch * seq
    pq, oq, pk, ok = _bias_placement()
    full = lambda a: pl.BlockSpec(a.shape, lambda b, i: (0, 0))
    row_blk = pl.BlockSpec((tc, LANE), lambda b, i: (b * nb + i, 0))
    return pl.pallas_call(
        functools.partial(_cumsum_kernel, tc=tc),
        grid=(batch, nb),
        in_specs=[row_blk, full(b_pad), full(pq), full(oq), full(pk), full(ok)],
        out_specs=[row_blk, row_blk],
        out_shape=[jax.ShapeDtypeStruct((T, LANE), bf16)] * 2,
        scratch_shapes=[pltpu.VMEM((1, LANE), f32)],
        compiler_params=pltpu.CompilerParams(
            dimension_semantics=("arbitrary", "arbitrary")),
        name="forget_cumsum",
    )(f2d, b_pad, pq, oq, pk, ok)


def _fox_kernel(q_ref, k_ref, v_ref, g_ref, qb_ref, kb_ref, o_ref,
                s_ref, p_ref, alpha_ref, m_ref, acc_ref, *, tq):
    qi = pl.program_id(2)
    lane = lax.broadcasted_iota(jnp.int32, (1, LANE), 1)
    lo = lane < HEAD_DIM
    own = (lo, jnp.logical_not(lo))
    head0 = 2 * pl.program_id(1)
    own_bias = [(lane >= (head0 + h) * BIAS_LANES) & (lane < (head0 + h + 1) * BIAS_LANES)
                for h in range(2)]
    q, qb = q_ref[0], qb_ref[...]
    zero = jnp.zeros_like(q)
    q2 = [jnp.concatenate([jnp.where(own[h], q, zero), jnp.where(own_bias[h], qb, zero)],
                          axis=1) for h in range(2)]
    rb = FOX_ROW_BLOCK
    row_in_blk = lax.broadcasted_iota(jnp.int32, (rb, LANE), 0)
    col = lax.broadcasted_iota(jnp.int32, (rb, LANE), 1)

    m_ref[...] = jnp.full(m_ref.shape, NEG, f32)
    acc_ref[...] = jnp.zeros(acc_ref.shape, f32)

    def scores(j, slot):
        r0 = pl.multiple_of(j * tq, tq)
        k2 = jnp.concatenate([k_ref[0, pl.ds(r0, tq), :], kb_ref[pl.ds(r0, tq), :]],
                             axis=1)
        for h in range(2):
            s_ref[slot, h] = lax.dot_general(q2[h], k2, (((1,), (1,)), ((), ())),
                                             preferred_element_type=f32)

    def softmax(slot, masked):
        for h in range(2):
            for r in range(tq // rb):
                rows = pl.ds(r * rb, rb)
                tiles = []
                for t in range(tq // LANE):
                    s = s_ref[slot, h, rows, t * LANE:(t + 1) * LANE]
                    if masked:
                        s = jnp.where(col + t * LANE <= row_in_blk + r * rb, s, NEG)
                    tiles.append(s)
                mx = functools.reduce(jnp.maximum, tiles)
                m_old = m_ref[h, rows, :]
                m_new = jnp.maximum(m_old, jnp.max(mx, axis=-1, keepdims=True))
                m_ref[h, rows, :] = m_new
                alpha_ref[slot, h, rows, :] = jnp.exp2(m_old - m_new)
                for t, s in enumerate(tiles):
                    p_ref[slot, h, rows, t * LANE:(t + 1) * LANE] = (
                        jnp.exp2(s - m_new).astype(bf16))

    def accumulate(j, slot):
        r0 = pl.multiple_of(j * tq, tq)
        v = v_ref[0, pl.ds(r0, tq), :]
        for h in range(2):
            vh = jnp.where(own[h], v, jnp.ones_like(v))
            acc_ref[h] = alpha_ref[slot, h] * acc_ref[h] + jnp.dot(
                p_ref[slot, h], vh, preferred_element_type=f32)

    def step_pair(jj, _):
        scores(2 * jj, 0)
        scores(2 * jj + 1, 1)
        softmax(0, False)
        accumulate(2 * jj, 0)
        softmax(1, False)
        accumulate(2 * jj + 1, 1)
        return 0

    def step_single(j, masked):
        scores(j, 0)
        softmax(0, masked)
        accumulate(j, 0)
        return 0

    lax.fori_loop(0, qi // 2, step_pair, 0)
    lax.fori_loop(0, qi % 2, lambda _, c: step_single(qi - 1, False), 0)
    step_single(qi, True)

    acc0, acc1 = acc_ref[0], acc_ref[1]
    num = jnp.where(lo, acc0, acc1)
    den = pltpu.roll(jnp.where(lo, acc1, acc0), HEAD_DIM, axis=1)
    o_ref[0] = (num / den * _silu(g_ref[0].astype(f32))).astype(bf16)


def _fox_attention(z, qb, kb, *, batch, seq, tq=512):
    tq = min(tq, seq)
    nq = seq // tq
    P = PAIRS_FOX
    T = batch * seq
    q_blk = lambda g0: pl.BlockSpec((1, tq, LANE), lambda b, p, i: (g0 + p, b * nq + i, 0))
    kv_blk = lambda g0: pl.BlockSpec((1, seq, LANE), lambda b, p, i: (g0 + p, b, 0))
    return pl.pallas_call(
        functools.partial(_fox_kernel, tq=tq),
        grid=(batch, P, nq),
        in_specs=[q_blk(0), kv_blk(P), kv_blk(2 * P), q_blk(3 * P),
                  pl.BlockSpec((tq, LANE), lambda b, p, i: (b * nq + i, 0)),
                  pl.BlockSpec((seq, LANE), lambda b, p, i: (b, 0))],
        out_specs=q_blk(0),
        out_shape=jax.ShapeDtypeStruct((P, T, LANE), bf16),
        scratch_shapes=[
            pltpu.VMEM((2, 2, tq, tq), f32),
            pltpu.VMEM((2, 2, tq, tq), bf16),
            pltpu.VMEM((2, 2, tq, LANE), f32),
            pltpu.VMEM((2, tq, LANE), f32),
            pltpu.VMEM((2, tq, LANE), f32),
        ],
        compiler_params=pltpu.CompilerParams(
            dimension_semantics=("arbitrary", "arbitrary", "arbitrary")),
        name="fox_attention",
    )(z, z, z, z, qb, kb)


def _dil_kernel(q_ref, kp_ref, kc_ref, vp_ref, vc_ref, g_ref, o_ref,
                qs_ref, ks_ref, vs_ref, q4_ref, k4_ref, v4_ref, os_ref, ls_ref):
    chunk = pl.program_id(2)
    CH, BLK, DEI = DIL_CHUNK, DIL_BLOCK, DIL_DEINTERLEAVE
    qs_ref[...] = q_ref[0].astype(f32)
    ks_ref[0:CH, :] = kp_ref[0].astype(f32)
    ks_ref[CH:2 * CH, :] = kc_ref[0].astype(f32)
    vs_ref[0:CH, :] = vp_ref[0].astype(f32)
    vs_ref[CH:2 * CH, :] = vc_ref[0].astype(f32)
    for res in range(DEI):
        nq, nk = CH // DEI, 2 * CH // DEI
        q4_ref[res * nq:(res + 1) * nq, :] = qs_ref[pl.ds(res, nq, stride=DEI), :]
        k4_ref[res * nk:(res + 1) * nk, :] = ks_ref[pl.ds(res, nk, stride=DEI), :]
        v4_ref[res * nk:(res + 1) * nk, :] = vs_ref[pl.ds(res, nk, stride=DEI), :]

    lane = lax.broadcasted_iota(jnp.int32, (1, LANE), 1)
    lo = lane < HEAD_DIM
    hi = jnp.logical_not(lo)
    qi = lax.broadcasted_iota(jnp.int32, (BLK, 2 * BLK), 0)
    kj = lax.broadcasted_iota(jnp.int32, (BLK, 2 * BLK), 1)
    dist = BLK + qi - kj
    n_tiles = CH // BLK

    for p, (window, r) in enumerate(DIL_PATTERNS):
        n_keys = window // r
        band = (dist >= 0) & (dist <= n_keys)
        bias_band = jnp.where(band, 0.0, NEG)
        bias_first = jnp.where(band & (kj >= BLK), 0.0, NEG)
        span = r * BLK

        def tile(t, r=r, span=span, p=p, bias_band=bias_band, bias_first=bias_first):
            sidx = t // r
            qstart = sidx * span + t % r
            kstart = CH + qstart - span
            no_prev = jnp.logical_and(chunk == 0, sidx == 0)
            bias = jnp.where(no_prev, bias_first, bias_band)
            if r % DEI == 0:
                rr = r // DEI
                res4, sub = (t % r) % DEI, (t % r) // DEI
                q0 = res4 * (CH // DEI) + sidx * (span // DEI) + sub
                k0 = res4 * (2 * CH // DEI) + CH // DEI + sidx * (span // DEI) + sub - span // DEI
                stride = rr if rr > 1 else None
                qt = q4_ref[pl.ds(q0, BLK, stride=stride), :].astype(bf16)
                kt = k4_ref[pl.ds(k0, 2 * BLK, stride=stride), :].astype(bf16)
                vt = v4_ref[pl.ds(k0, 2 * BLK, stride=stride), :].astype(bf16)
            else:
                qt = qs_ref[pl.ds(qstart, BLK, stride=r), :].astype(bf16)
                kt = ks_ref[pl.ds(kstart, 2 * BLK, stride=r), :].astype(bf16)
                vt = vs_ref[pl.ds(kstart, 2 * BLK, stride=r), :].astype(bf16)
            res, ms = [], []
            for own in (lo, hi):
                qh = jnp.where(own, qt, jnp.zeros_like(qt))
                vh = jnp.where(own, vt, jnp.ones_like(vt))
                s = lax.dot_general(qh, kt, (((1,), (1,)), ((), ())),
                                    preferred_element_type=f32) + bias
                m = jnp.max(s, axis=-1, keepdims=True)
                e = jnp.exp2(s - m).astype(bf16)
                res.append(jnp.dot(e, vh, preferred_element_type=f32))
                ms.append(m)
            num = jnp.where(lo, res[0], res[1])
            den = pltpu.roll(jnp.where(lo, res[1], res[0]), HEAD_DIM, axis=1)
            os_ref[p, pl.ds(qstart, BLK, stride=r), :] = num / den
            ls_ref[p, pl.ds(qstart, BLK, stride=r), :] = (
                jnp.where(lo, ms[0], ms[1]) + jnp.log2(den))

        def group(g, _, tile=tile):
            for u in range(DIL_UNROLL):
                tile(g * DIL_UNROLL + u)
            return 0

        lax.fori_loop(0, n_tiles // DIL_UNROLL, group, 0)

    l0, l1, l2 = ls_ref[0], ls_ref[1], ls_ref[2]
    mx = jnp.maximum(jnp.maximum(l0, l1), l2)
    w0, w1, w2 = jnp.exp2(l0 - mx), jnp.exp2(l1 - mx), jnp.exp2(l2 - mx)
    y = (w0 * os_ref[0] + w1 * os_ref[1] + w2 * os_ref[2]) / (w0 + w1 + w2)
    o_ref[0] = (y * _silu(g_ref[0].astype(f32))).astype(bf16)


def _dilated_attention(z, *, batch, seq, group0):
    CH = DIL_CHUNK
    assert seq % CH == 0
    nc = seq // CH
    P = PAIRS_DIL
    T = batch * seq
    g0 = group0

    def cur(off):
        return lambda b, p, c: (g0 + off * P + p, b * nc + c, 0)

    def prev(off):
        return lambda b, p, c: (g0 + off * P + p, b * nc + jnp.maximum(c - 1, 0), 0)

    blk = (1, CH, LANE)
    return pl.pallas_call(
        _dil_kernel,
        grid=(batch, P, nc),
        in_specs=[
            pl.BlockSpec(blk, cur(0)),
            pl.BlockSpec(blk, prev(1)), pl.BlockSpec(blk, cur(1)),
            pl.BlockSpec(blk, prev(2)), pl.BlockSpec(blk, cur(2)),
            pl.BlockSpec(blk, cur(3)),
        ],
        out_specs=pl.BlockSpec(blk, lambda b, p, c: (p, b * nc + c, 0)),
        out_shape=jax.ShapeDtypeStruct((P, T, LANE), bf16),
        scratch_shapes=[
            pltpu.VMEM((CH, LANE), f32),
            pltpu.VMEM((2 * CH, LANE), f32),
            pltpu.VMEM((2 * CH, LANE), f32),
            pltpu.VMEM((CH, LANE), f32),
            pltpu.VMEM((2 * CH, LANE), f32),
            pltpu.VMEM((2 * CH, LANE), f32),
            pltpu.VMEM((len(DIL_PATTERNS), CH, LANE), f32),
            pltpu.VMEM((len(DIL_PATTERNS), CH, LANE), f32),
        ],
        compiler_params=pltpu.CompilerParams(
            dimension_semantics=("arbitrary", "arbitrary", "arbitrary")),
        name="dilated_attention",
    )(z, z, z, z, z, z)


def _ret_kernel(q_ref, k_ref, v_ref, g_ref, intra_ref, qd_ref, kd_ref, cd_ref,
                o_ref, state_ref, *, C, n_chunks):
    @pl.when(pl.program_id(2) == 0)
    def _():
        state_ref[...] = jnp.zeros_like(state_ref)

    nq = RET_QK_DIM // LANE
    nv = RET_V_DIM // LANE
    intra = intra_ref[0]
    qd = qd_ref[0]
    kd = kd_ref[0]
    cd = cd_ref[0]

    st = state_ref[...]
    for ci in range(n_chunks):
        r0 = ci * C
        q = jnp.concatenate([q_ref[c, pl.ds(r0, C), :] for c in range(nq)], axis=1)
        k = jnp.concatenate([k_ref[c, pl.ds(r0, C), :] for c in range(nq)], axis=1)
        v = jnp.concatenate([v_ref[c, pl.ds(r0, C), :] for c in range(nv)], axis=1)
        s = lax.dot_general(q, k, (((1,), (1,)), ((), ())),
                            preferred_element_type=f32) * intra
        inner = jnp.dot(s.astype(bf16), v, preferred_element_type=f32)
        cross = jnp.dot(q, st.astype(bf16), preferred_element_type=f32) * qd
        kdec = (k.astype(f32) * kd).astype(bf16)
        upd = lax.dot_general(kdec, v, (((0,), (0,)), ((), ())),
                              preferred_element_type=f32)
        st = st * cd + upd
        if ci == n_chunks - 1:
            state_ref[...] = st
        y = inner + cross
        mu = jnp.mean(y, axis=-1, keepdims=True)
        yc = y - mu
        var = jnp.mean(yc * yc, axis=-1, keepdims=True)
        yn = yc * lax.rsqrt(var + EPS)
        g = jnp.concatenate([g_ref[c, pl.ds(r0, C), :] for c in range(nv)],
                            axis=1).astype(f32)
        out = (yn * _silu(g)).astype(bf16)
        for c in range(nv):
            o_ref[c, pl.ds(r0, C), :] = out[:, c * LANE:(c + 1) * LANE]


def _retention(z, *, batch, seq, C=RET_CHUNK, tr=1024):
    tr = min(tr, seq)
    nr = seq // tr
    H = N_HEADS_RET
    T = batch * seq
    nq = RET_QK_DIM // LANE
    nv = RET_V_DIM // LANE
    kq0 = H
    v0 = 2 * H * nq // nv
    g0 = v0 + H

    log_gamma = jnp.log1p(-jnp.power(2.0, -5.0 - jnp.arange(H, dtype=f32)))
    pos = jnp.arange(C, dtype=f32)
    rel = pos[:, None] - pos[None, :]
    intra = jnp.where(rel >= 0,
                      jnp.exp(log_gamma[:, None, None] * jnp.maximum(rel, 0.0)), 0.0)
    q_decay = jnp.exp(log_gamma[:, None] * (pos + 1.0))[:, :, None]
    k_decay = jnp.exp(log_gamma[:, None] * (C - 1.0 - pos))[:, :, None]
    chunk_decay = jnp.exp(log_gamma * C)[:, None, None]

    return pl.pallas_call(
        functools.partial(_ret_kernel, C=C, n_chunks=tr // C),
        grid=(batch, H, nr),
        in_specs=[
            pl.BlockSpec((nq, tr, LANE), lambda b, h, i: (h, b * nr + i, 0)),
            pl.BlockSpec((nq, tr, LANE), lambda b, h, i: (kq0 + h, b * nr + i, 0)),
            pl.BlockSpec((nv, tr, LANE), lambda b, h, i: (v0 + h, b * nr + i, 0)),
            pl.BlockSpec((nv, tr, LANE), lambda b, h, i: (g0 + h, b * nr + i, 0)),
            pl.BlockSpec((1, C, C), lambda b, h, i: (h, 0, 0)),
            pl.BlockSpec((1, C, 1), lambda b, h, i: (h, 0, 0)),
            pl.BlockSpec((1, C, 1), lambda b, h, i: (h, 0, 0)),
            pl.BlockSpec((1, 1, 1), lambda b, h, i: (h, 0, 0)),
        ],
        out_specs=pl.BlockSpec((nv, tr, LANE), lambda b, h, i: (h, b * nr + i, 0)),
        out_shape=jax.ShapeDtypeStruct((H * nv, T, LANE), bf16),
        scratch_shapes=[pltpu.VMEM((RET_QK_DIM, RET_V_DIM), f32)],
        compiler_params=pltpu.CompilerParams(
            dimension_semantics=("arbitrary", "arbitrary", "arbitrary")),
        name="retention",
    )(z, z, z, z, intra, q_decay, k_decay, chunk_decay)


def _out_proj_kernel(*refs, n_y, final_norm):
    y_refs = refs[:n_y]
    w_ref, x_ref = refs[n_y], refs[n_y + 1]
    g_ref = refs[n_y + 2] if final_norm else None
    o_ref = refs[-1]
    y = jnp.concatenate([r[c] for r in y_refs for c in range(r.shape[0])], axis=1)
    out = x_ref[...] + jnp.dot(y, w_ref[...], preferred_element_type=f32)
    if final_norm:
        ms = jnp.mean(out * out, axis=-1, keepdims=True)
        out = out * lax.rsqrt(ms + EPS) * g_ref[...]
    o_ref[...] = out


def _out_proj(ys, w, x2d, *, gain=None, tm=512, name):
    T, D = x2d.shape
    tm = min(tm, T)
    in_specs = [pl.BlockSpec((y.shape[0], tm, LANE), lambda i: (0, i, 0)) for y in ys]
    in_specs += [pl.BlockSpec(w.shape, lambda i: (0, 0)),
                 pl.BlockSpec((tm, D), lambda i: (i, 0))]
    args = list(ys) + [w, x2d]
    if gain is not None:
        in_specs.append(pl.BlockSpec((1, D), lambda i: (0, 0)))
        args.append(gain.reshape(1, D).astype(f32))
    return pl.pallas_call(
        functools.partial(_out_proj_kernel, n_y=len(ys), final_norm=gain is not None),
        grid=(T // tm,),
        in_specs=in_specs,
        out_specs=pl.BlockSpec((tm, D), lambda i: (i, 0)),
        out_shape=jax.ShapeDtypeStruct((T, D), f32),
        compiler_params=pltpu.CompilerParams(dimension_semantics=("arbitrary",)),
        name=name,
    )(*args)


def _even_layer(x2d, g_norm, w_in, b_f, w_out, *, batch, seq):
    D = x2d.shape[1]
    a_w = 4 * WIDTH_FOX
    nf = N_HEADS_FOX
    scale = HEAD_DIM ** -0.5
    col_scale = jnp.ones((a_w + 4 * WIDTH_DIL,), f32)
    col_scale = col_scale.at[:WIDTH_FOX].set(scale * LOG2E)
    col_scale = col_scale.at[a_w:a_w + WIDTH_DIL].set(scale * LOG2E)
    w_main = jnp.concatenate([w_in[:, :a_w], w_in[:, a_w + nf:]], axis=1) * col_scale
    w_f = jnp.pad(w_in[:, a_w:a_w + nf], ((0, 0), (0, LANE - nf)))
    z, f2d = _in_proj(x2d, g_norm, w_main.astype(bf16), seq=seq, wf=w_f.astype(bf16),
                      name="even_in_proj")
    b_pad = jnp.pad(b_f.astype(f32), (0, LANE - nf)).reshape(1, LANE)
    qb, kb = _forget_cumsum(f2d, b_pad, batch=batch, seq=seq)
    ya = _fox_attention(z, qb, kb, batch=batch, seq=seq)
    yb = _dilated_attention(z, batch=batch, seq=seq, group0=4 * PAIRS_FOX)
    return _out_proj([ya, yb], w_out.astype(bf16), x2d, name="even_out_proj")


def _odd_layer(x2d, g_norm, w_in, w_out, *, batch, seq, final_gain):
    H, dk = N_HEADS_RET, RET_QK_DIM
    qk_w = H * dk
    half = dk // 2
    perm = (jnp.arange(H)[:, None, None] * dk
            + jnp.arange(2)[None, :, None]
            + 2 * jnp.arange(half)[None, None, :]).reshape(-1)
    w_q = w_in[:, :qk_w][:, perm]
    w_k = w_in[:, qk_w:2 * qk_w][:, perm] * (dk ** -0.5)
    w_all = jnp.concatenate([w_q, w_k, w_in[:, 2 * qk_w:]], axis=1).astype(bf16)

    inv = 1.0 / (ROT_BASE ** jnp.linspace(0.0, 1.0, half, dtype=f32))
    ang = jnp.arange(seq, dtype=f32)[:, None] * inv[None, :]
    rot = (jnp.cos(ang), jnp.sin(ang))
    (z,) = _in_proj(x2d, g_norm, w_all, seq=seq, rot=rot, n_rot_blocks=2 * qk_w // 1024,
                    name="odd_in_proj")
    yc = _retention(z, batch=batch, seq=seq)
    return _out_proj([yc], w_out.astype(bf16), x2d, gain=final_gain, name="odd_out_proj")


def kernel(x, even_norm, even_w_in, even_b_f, even_w_out, odd_norm, odd_w_in, odd_w_out, final_norm):
    batch, seq, D = x.shape
    x2d = x.reshape(batch * seq, D)
    x2d = _even_layer(x2d, even_norm[0], even_w_in[0], even_b_f[0], even_w_out[0],
                      batch=batch, seq=seq)
    out = _odd_layer(x2d, odd_norm[0], odd_w_in[0], odd_w_out[0],
                     batch=batch, seq=seq, final_gain=final_norm)
    return out.reshape(batch, seq, D)
```

```python
import functools

import numpy as np
import jax
import jax.numpy as jnp
from jax import lax
from jax.experimental import pallas as pl
from jax.experimental.pallas import tpu as pltpu

f32 = jnp.float32
bf16 = jnp.bfloat16

LANE = 128
SUBLANE = 8
HEAD_DIM = 64
N_HEADS_FOX = 16
N_HEADS_DIL = 16
WIDTH_FOX = N_HEADS_FOX * HEAD_DIM
WIDTH_DIL = N_HEADS_DIL * HEAD_DIM
DIL_BLOCK = 128
DIL_PATTERNS = ((128, 1), (512, 4), (2048, 16))
DIL_CHUNK = 2048
DIL_DEINTERLEAVE = 4
DIL_UNROLL = 8
N_HEADS_RET = 4
RET_QK_DIM = 256
RET_V_DIM = 512
RET_CHUNK = 256
ROT_BASE = 10000.0
EPS = 1e-6
NEG = -1e30

LOG2E = 1.4426950408889634
N_PIECES = 3
BIAS_LANES = 8
FOX_KEY_CHUNK = 128

PAIRS_FOX = WIDTH_FOX // LANE
PAIRS_DIL = WIDTH_DIL // LANE


def _silu(g):
    return g / (1.0 + jnp.exp(-g))


def _in_proj_kernel(*refs, n_groups, with_forget, n_rot_blocks):
    it = iter(refs)
    x_ref, g_ref, w_ref = next(it), next(it), next(it)
    wf_ref = next(it) if with_forget else None
    cos_ref = sin_ref = None
    if n_rot_blocks:
        cos_ref, sin_ref = next(it), next(it)
    o_ref = next(it)
    f_ref = next(it) if with_forget else None
    h_ref = next(it)

    j = pl.program_id(1)

    @pl.when(j == 0)
    def _():
        x = x_ref[...]
        ms = jnp.mean(x * x, axis=-1, keepdims=True)
        h = (x * lax.rsqrt(ms + EPS) * g_ref[...]).astype(bf16)
        h_ref[...] = h
        if with_forget:
            f_ref[...] = jnp.dot(h, wf_ref[...], preferred_element_type=f32)

    acc = jnp.dot(h_ref[...], w_ref[...], preferred_element_type=f32)

    def store_plain():
        for c in range(n_groups):
            o_ref[c] = acc[:, c * LANE:(c + 1) * LANE].astype(bf16)

    if not n_rot_blocks:
        store_plain()
        return

    @pl.when(j < n_rot_blocks)
    def _():
        cos, sin = cos_ref[...], sin_ref[...]
        for c in range(0, n_groups, 2):
            x1 = acc[:, c * LANE:(c + 1) * LANE]
            x2 = acc[:, (c + 1) * LANE:(c + 2) * LANE]
            o_ref[c] = (x1 * cos - x2 * sin).astype(bf16)
            o_ref[c + 1] = (x1 * sin + x2 * cos).astype(bf16)

    @pl.when(j >= n_rot_blocks)
    def _():
        store_plain()


def _in_proj(x2d, gain, w, *, seq, wf=None, rot=None, n_rot_blocks=0, tm=1024, bn=1024, name):
    T, D = x2d.shape
    N = w.shape[1]
    tm = min(tm, seq)
    n_groups = bn // LANE
    grid = (T // tm, N // bn)
    in_specs = [
        pl.BlockSpec((tm, D), lambda i, j: (i, 0)),
        pl.BlockSpec((1, D), lambda i, j: (0, 0)),
        pl.BlockSpec((D, bn), lambda i, j: (0, j)),
    ]
    args = [x2d, gain.reshape(1, D).astype(f32), w]
    out_shape = [jax.ShapeDtypeStruct((N // LANE, T, LANE), bf16)]
    out_specs = [pl.BlockSpec((n_groups, tm, LANE), lambda i, j: (j, i, 0))]
    if wf is not None:
        in_specs.append(pl.BlockSpec((D, LANE), lambda i, j: (0, 0)))
        args.append(wf)
        out_shape.append(jax.ShapeDtypeStruct((T, LANE), f32))
        out_specs.append(pl.BlockSpec((tm, LANE), lambda i, j: (i, 0)))
    if n_rot_blocks:
        spb = seq // tm
        for t in rot:
            in_specs.append(pl.BlockSpec((tm, LANE), lambda i, j: (i % spb, 0)))
            args.append(t)
    kern = functools.partial(_in_proj_kernel, n_groups=n_groups,
                             with_forget=wf is not None, n_rot_blocks=n_rot_blocks)
    outs = pl.pallas_call(
        kern,
        grid=grid,
        in_specs=in_specs,
        out_specs=out_specs,
        out_shape=out_shape,
        scratch_shapes=[pltpu.VMEM((tm, D), bf16)],
        compiler_params=pltpu.CompilerParams(
            dimension_semantics=("arbitrary", "arbitrary")),
        name=name,
    )(*args)
    return outs


def _split_bf16(x):
    p0 = x.astype(bf16)
    r1 = x - p0.astype(f32)
    p1 = r1.astype(bf16)
    p2 = (r1 - p1.astype(f32)).astype(bf16)
    return jnp.concatenate([p0, p1, p2], axis=1)


def _cumsum_kernel(f_ref, b_ref, pq_ref, oq_ref, pk_ref, ok_ref, qb_ref, kb_ref,
                   carry_ref, *, tc):
    @pl.when(pl.program_id(1) == 0)
    def _():
        carry_ref[...] = jnp.zeros_like(carry_ref)

    z = f_ref[...] + b_ref[...]
    lf = jnp.minimum(z, 0.0) - jnp.log1p(jnp.exp(-jnp.abs(z)))
    row = lax.broadcasted_iota(jnp.int32, (tc, tc), 0)
    col = lax.broadcasted_iota(jnp.int32, (tc, tc), 1)
    tri = jnp.where(col <= row, 1.0, 0.0).astype(bf16)
    cs = jnp.dot(tri, _split_bf16(lf), preferred_element_type=f32)
    c = (cs[:, :LANE] + cs[:, LANE:2 * LANE] + cs[:, 2 * LANE:]) + carry_ref[...]
    carry_ref[...] = c[tc - 1:tc, :]

    pieces = _split_bf16(c * LOG2E)
    qb = jnp.dot(pieces, pq_ref[...], preferred_element_type=f32) + oq_ref[...]
    kb = jnp.dot(pieces, pk_ref[...], preferred_element_type=f32) + ok_ref[...]
    qb_ref[...] = qb.astype(bf16)
    kb_ref[...] = kb.astype(bf16)


def _bias_placement():
    assert N_HEADS_FOX * BIAS_LANES == LANE and 2 * N_PIECES <= BIAS_LANES
    pq = np.zeros((N_PIECES * LANE, LANE), np.float32)
    pk = np.zeros((N_PIECES * LANE, LANE), np.float32)
    oq = np.zeros((1, LANE), np.float32)
    ok = np.zeros((1, LANE), np.float32)
    for head in range(N_HEADS_FOX):
        base = BIAS_LANES * head
        for piece in range(N_PIECES):
            pq[piece * LANE + head, base + piece] = 1.0
            ok[0, base + piece] = 1.0
            oq[0, base + N_PIECES + piece] = 1.0
            pk[piece * LANE + head, base + N_PIECES + piece] = -1.0
    return (jnp.asarray(pq, bf16), jnp.asarray(oq, f32),
            jnp.asarray(pk, bf16), jnp.asarray(ok, f32))


def _forget_cumsum(f2d, b_pad, *, batch, seq, tc=512):
    tc = min(tc, seq)
    nb = seq // tc
    T = batch * seq
    pq, oq, pk, ok = _bias_placement()
    full = lambda a: pl.BlockSpec(a.shape, lambda b, i: (0, 0))
    row_blk = pl.BlockSpec((tc, LANE), lambda b, i: (b * nb + i, 0))
    return pl.pallas_call(
        functools.partial(_cumsum_kernel, tc=tc),
        grid=(batch, nb),
        in_specs=[row_blk, full(b_pad), full(pq), full(oq), full(pk), full(ok)],
        out_specs=[row_blk, row_blk],
        out_shape=[jax.ShapeDtypeStruct((T, LANE), bf16)] * 2,
        scratch_shapes=[pltpu.VMEM((1, LANE), f32)],
        compiler_params=pltpu.CompilerParams(
            dimension_semantics=("arbitrary", "arbitrary")),
        name="forget_cumsum",
    )(f2d, b_pad, pq, oq, pk, ok)


def _fox_kernel(q_ref, k_ref, v_ref, g_ref, qb_ref, kb_ref, o_ref,
                vt_ref, s_ref, p_ref, alpha_ref, mx_ref, m_ref, acc_ref, *, tq, n_blocks):
    qi = pl.program_id(2)
    lane = lax.broadcasted_iota(jnp.int32, (1, LANE), 1)
    lo = lane < HEAD_DIM
    own = (lo, jnp.logical_not(lo))
    head0 = 2 * pl.program_id(1)
    own_bias = [(lane >= (head0 + h) * BIAS_LANES) & (lane < (head0 + h + 1) * BIAS_LANES)
                for h in range(2)]
    q, qb = q_ref[0], qb_ref[...]
    zero = jnp.zeros_like(q)
    q2 = [jnp.concatenate([jnp.where(own[h], q, zero), jnp.where(own_bias[h], qb, zero)],
                          axis=1) for h in range(2)]
    kc = FOX_KEY_CHUNK
    causal = (lax.broadcasted_iota(jnp.int32, (tq, tq), 0)
              <= lax.broadcasted_iota(jnp.int32, (tq, tq), 1))

    @pl.when(qi == 0)
    def _():
        for j in range(n_blocks):
            v = v_ref[0, j * tq:(j + 1) * tq, :]
            for h in range(2):
                vh = jnp.where(own[h], v, jnp.ones_like(v))
                vt_ref[h, j] = vh.astype(f32).T.astype(bf16)

    m_ref[...] = jnp.full(m_ref.shape, NEG, f32)
    acc_ref[...] = jnp.zeros(acc_ref.shape, f32)

    def scores(j, slot, masked):
        r0 = pl.multiple_of(j * tq, tq)
        k2 = jnp.concatenate([k_ref[0, pl.ds(r0, tq), :], kb_ref[pl.ds(r0, tq), :]],
                             axis=1)
        for h in range(2):
            s = lax.dot_general(k2, q2[h], (((1,), (1,)), ((), ())),
                                preferred_element_type=f32)
            if masked:
                s = jnp.where(causal, s, NEG)
            s_ref[slot, h] = s
            mx_ref[slot, h] = jnp.max(s.reshape(tq // SUBLANE, SUBLANE, tq), axis=0)

    def softmax(slot):
        for h in range(2):
            for t in range(tq // LANE):
                cols = slice(t * LANE, (t + 1) * LANE)
                m_old = m_ref[h, :, cols]
                m_new = jnp.maximum(
                    m_old, jnp.max(mx_ref[slot, h, :, cols], axis=0, keepdims=True))
                m_ref[h, :, cols] = m_new
                alpha_ref[slot, h, :, cols] = jnp.exp2(m_old - m_new)
                for c in range(tq // kc):
                    rows = slice(c * kc, (c + 1) * kc)
                    p_ref[slot, h, rows, cols] = jnp.exp2(
                        s_ref[slot, h, rows, cols] - m_new).astype(bf16)

    def accumulate(j, slot):
        for h in range(2):
            acc_ref[h] = alpha_ref[slot, h] * acc_ref[h] + jnp.dot(
                vt_ref[h, j], p_ref[slot, h], preferred_element_type=f32)

    def step_pair(jj, _):
        scores(2 * jj, 0, False)
        scores(2 * jj + 1, 1, False)
        softmax(0)
        accumulate(2 * jj, 0)
        softmax(1)
        accumulate(2 * jj + 1, 1)
        return 0

    def step_single(j, masked):
        scores(j, 0, masked)
        softmax(0)
        accumulate(j, 0)
        return 0

    lax.fori_loop(0, qi // 2, step_pair, 0)
    lax.fori_loop(0, qi % 2, lambda _, c: step_single(qi - 1, False), 0)
    step_single(qi, True)

    acc0, acc1 = acc_ref[0], acc_ref[1]
    y_t = jnp.concatenate(
        [acc0[:HEAD_DIM] / acc0[HEAD_DIM:HEAD_DIM + 1], acc1[HEAD_DIM:] / acc1[:1]], axis=0)
    o_ref[0] = (y_t.T * _silu(g_ref[0].astype(f32))).astype(bf16)


def _fox_attention(z, qb, kb, *, batch, seq, tq=512):
    tq = min(tq, seq)
    nq = seq // tq
    P = PAIRS_FOX
    T = batch * seq
    q_blk = lambda g0: pl.BlockSpec((1, tq, LANE), lambda b, p, i: (g0 + p, b * nq + i, 0))
    kv_blk = lambda g0: pl.BlockSpec((1, seq, LANE), lambda b, p, i: (g0 + p, b, 0))
    return pl.pallas_call(
        functools.partial(_fox_kernel, tq=tq, n_blocks=nq),
        grid=(batch, P, nq),
        in_specs=[q_blk(0), kv_blk(P), kv_blk(2 * P), q_blk(3 * P),
                  pl.BlockSpec((tq, LANE), lambda b, p, i: (b * nq + i, 0)),
                  pl.BlockSpec((seq, LANE), lambda b, p, i: (b, 0))],
        out_specs=q_blk(0),
        out_shape=jax.ShapeDtypeStruct((P, T, LANE), bf16),
        scratch_shapes=[
            pltpu.VMEM((2, nq, LANE, tq), bf16),
            pltpu.VMEM((2, 2, tq, tq), f32),
            pltpu.VMEM((2, 2, tq, tq), bf16),
            pltpu.VMEM((2, 2, 1, tq), f32),
            pltpu.VMEM((2, 2, SUBLANE, tq), f32),
            pltpu.VMEM((2, 1, tq), f32),
            pltpu.VMEM((2, LANE, tq), f32),
        ],
        compiler_params=pltpu.CompilerParams(
            dimension_semantics=("arbitrary", "arbitrary", "arbitrary")),
        name="fox_attention",
    )(z, z, z, z, qb, kb)


def _dil_kernel(q_ref, kp_ref, kc_ref, vp_ref, vc_ref, g_ref, o_ref,
                qs_ref, ks_ref, vs_ref, q4_ref, k4_ref, v4_ref, os_ref, ls_ref):
    chunk = pl.program_id(2)
    CH, BLK, DEI = DIL_CHUNK, DIL_BLOCK, DIL_DEINTERLEAVE
    qs_ref[...] = q_ref[0].astype(f32)
    ks_ref[0:CH, :] = kp_ref[0].astype(f32)
    ks_ref[CH:2 * CH, :] = kc_ref[0].astype(f32)
    vs_ref[0:CH, :] = vp_ref[0].astype(f32)
    vs_ref[CH:2 * CH, :] = vc_ref[0].astype(f32)
    for res in range(DEI):
        nq, nk = CH // DEI, 2 * CH // DEI
        q4_ref[res * nq:(res + 1) * nq, :] = qs_ref[pl.ds(res, nq, stride=DEI), :]
        k4_ref[res * nk:(res + 1) * nk, :] = ks_ref[pl.ds(res, nk, stride=DEI), :]
        v4_ref[res * nk:(res + 1) * nk, :] = vs_ref[pl.ds(res, nk, stride=DEI), :]

    lane = lax.broadcasted_iota(jnp.int32, (1, LANE), 1)
    lo = lane < HEAD_DIM
    hi = jnp.logical_not(lo)
    qi = lax.broadcasted_iota(jnp.int32, (BLK, 2 * BLK), 0)
    kj = lax.broadcasted_iota(jnp.int32, (BLK, 2 * BLK), 1)
    dist = BLK + qi - kj
    n_tiles = CH // BLK

    for p, (window, r) in enumerate(DIL_PATTERNS):
        n_keys = window // r
        band = (dist >= 0) & (dist <= n_keys)
        bias_band = jnp.where(band, 0.0, NEG)
        bias_first = jnp.where(band & (kj >= BLK), 0.0, NEG)
        span = r * BLK

        def tile(t, r=r, span=span, p=p, bias_band=bias_band, bias_first=bias_first):
            sidx = t // r
            qstart = sidx * span + t % r
            kstart = CH + qstart - span
            no_prev = jnp.logical_and(chunk == 0, sidx == 0)
            bias = jnp.where(no_prev, bias_first, bias_band)
            if r % DEI == 0:
                rr = r // DEI
                res4, sub = (t % r) % DEI, (t % r) // DEI
                q0 = res4 * (CH // DEI) + sidx * (span // DEI) + sub
                k0 = res4 * (2 * CH // DEI) + CH // DEI + sidx * (span // DEI) + sub - span // DEI
                stride = rr if rr > 1 else None
                qt = q4_ref[pl.ds(q0, BLK, stride=stride), :].astype(bf16)
                kt = k4_ref[pl.ds(k0, 2 * BLK, stride=stride), :].astype(bf16)
                vt = v4_ref[pl.ds(k0, 2 * BLK, stride=stride), :].astype(bf16)
            else:
                qt = qs_ref[pl.ds(qstart, BLK, stride=r), :].astype(bf16)
                kt = ks_ref[pl.ds(kstart, 2 * BLK, stride=r), :].astype(bf16)
                vt = vs_ref[pl.ds(kstart, 2 * BLK, stride=r), :].astype(bf16)
            res, ms = [], []
            for own in (lo, hi):
                qh = jnp.where(own, qt, jnp.zeros_like(qt))
                vh = jnp.where(own, vt, jnp.ones_like(vt))
                s = lax.dot_general(qh, kt, (((1,), (1,)), ((), ())),
                                    preferred_element_type=f32) + bias
                m = jnp.max(s, axis=-1, keepdims=True)
                e = jnp.exp2(s - m).astype(bf16)
                res.append(jnp.dot(e, vh, preferred_element_type=f32))
                ms.append(m)
            num = jnp.where(lo, res[0], res[1])
            den = pltpu.roll(jnp.where(lo, res[1], res[0]), HEAD_DIM, axis=1)
            os_ref[p, pl.ds(qstart, BLK, stride=r), :] = num / den
            ls_ref[p, pl.ds(qstart, BLK, stride=r), :] = (
                jnp.where(lo, ms[0], ms[1]) + jnp.log2(den))

        def group(g, _, tile=tile):
            for u in range(DIL_UNROLL):
                tile(g * DIL_UNROLL + u)
            return 0

        lax.fori_loop(0, n_tiles // DIL_UNROLL, group, 0)

    l0, l1, l2 = ls_ref[0], ls_ref[1], ls_ref[2]
    mx = jnp.maximum(jnp.maximum(l0, l1), l2)
    w0, w1, w2 = jnp.exp2(l0 - mx), jnp.exp2(l1 - mx), jnp.exp2(l2 - mx)
    y = (w0 * os_ref[0] + w1 * os_ref[1] + w2 * os_ref[2]) / (w0 + w1 + w2)
    o_ref[0] = (y * _silu(g_ref[0].astype(f32))).astype(bf16)


def _dilated_attention(z, *, batch, seq, group0):
    CH = DIL_CHUNK
    assert seq % CH == 0
    nc = seq // CH
    P = PAIRS_DIL
    T = batch * seq
    g0 = group0

    def cur(off):
        return lambda b, p, c: (g0 + off * P + p, b * nc + c, 0)

    def prev(off):
        return lambda b, p, c: (g0 + off * P + p, b * nc + jnp.maximum(c - 1, 0), 0)

    blk = (1, CH, LANE)
    return pl.pallas_call(
        _dil_kernel,
        grid=(batch, P, nc),
        in_specs=[
            pl.BlockSpec(blk, cur(0)),
            pl.BlockSpec(blk, prev(1)), pl.BlockSpec(blk, cur(1)),
            pl.BlockSpec(blk, prev(2)), pl.BlockSpec(blk, cur(2)),
            pl.BlockSpec(blk, cur(3)),
        ],
        out_specs=pl.BlockSpec(blk, lambda b, p, c: (p, b * nc + c, 0)),
        out_shape=jax.ShapeDtypeStruct((P, T, LANE), bf16),
        scratch_shapes=[
            pltpu.VMEM((CH, LANE), f32),
            pltpu.VMEM((2 * CH, LANE), f32),
            pltpu.VMEM((2 * CH, LANE), f32),
            pltpu.VMEM((CH, LANE), f32),
            pltpu.VMEM((2 * CH, LANE), f32),
            pltpu.VMEM((2 * CH, LANE), f32),
            pltpu.VMEM((len(DIL_PATTERNS), CH, LANE), f32),
            pltpu.VMEM((len(DIL_PATTERNS), CH, LANE), f32),
        ],
        compiler_params=pltpu.CompilerParams(
            dimension_semantics=("arbitrary", "arbitrary", "arbitrary")),
        name="dilated_attention",
    )(z, z, z, z, z, z)


def _ret_kernel(q_ref, k_ref, v_ref, g_ref, intra_ref, qd_ref, kd_ref, cd_ref,
                o_ref, state_ref, *, C, n_chunks):
    @pl.when(pl.program_id(2) == 0)
    def _():
        state_ref[...] = jnp.zeros_like(state_ref)

    nq = RET_QK_DIM // LANE
    nv = RET_V_DIM // LANE
    intra = intra_ref[0]
    qd = qd_ref[0]
    kd = kd_ref[0]
    cd = cd_ref[0]

    st = state_ref[...]
    for ci in range(n_chunks):
        r0 = ci * C
        q = jnp.concatenate([q_ref[c, pl.ds(r0, C), :] for c in range(nq)], axis=1)
        k = jnp.concatenate([k_ref[c, pl.ds(r0, C), :] for c in range(nq)], axis=1)
        v = jnp.concatenate([v_ref[c, pl.ds(r0, C), :] for c in range(nv)], axis=1)
        s = lax.dot_general(q, k, (((1,), (1,)), ((), ())),
                            preferred_element_type=f32) * intra
        inner = jnp.dot(s.astype(bf16), v, preferred_element_type=f32)
        cross = jnp.dot(q, st.astype(bf16), preferred_element_type=f32) * qd
        kdec = (k.astype(f32) * kd).astype(bf16)
        upd = lax.dot_general(kdec, v, (((0,), (0,)), ((), ())),
                              preferred_element_type=f32)
        st = st * cd + upd
        if ci == n_chunks - 1:
            state_ref[...] = st
        y = inner + cross
        mu = jnp.mean(y, axis=-1, keepdims=True)
        yc = y - mu
        var = jnp.mean(yc * yc, axis=-1, keepdims=True)
        yn = yc * lax.rsqrt(var + EPS)
        g = jnp.concatenate([g_ref[c, pl.ds(r0, C), :] for c in range(nv)],
                            axis=1).astype(f32)
        out = (yn * _silu(g)).astype(bf16)
        for c in range(nv):
            o_ref[c, pl.ds(r0, C), :] = out[:, c * LANE:(c + 1) * LANE]


def _retention(z, *, batch, seq, C=RET_CHUNK, tr=1024):
    tr = min(tr, seq)
    nr = seq // tr
    H = N_HEADS_RET
    T = batch * seq
    nq = RET_QK_DIM // LANE
    nv = RET_V_DIM // LANE
    kq0 = H
    v0 = 2 * H * nq // nv
    g0 = v0 + H

    log_gamma = jnp.log1p(-jnp.power(2.0, -5.0 - jnp.arange(H, dtype=f32)))
    pos = jnp.arange(C, dtype=f32)
    rel = pos[:, None] - pos[None, :]
    intra = jnp.where(rel >= 0,
                      jnp.exp(log_gamma[:, None, None] * jnp.maximum(rel, 0.0)), 0.0)
    q_decay = jnp.exp(log_gamma[:, None] * (pos + 1.0))[:, :, None]
    k_decay = jnp.exp(log_gamma[:, None] * (C - 1.0 - pos))[:, :, None]
    chunk_decay = jnp.exp(log_gamma * C)[:, None, None]

    return pl.pallas_call(
        functools.partial(_ret_kernel, C=C, n_chunks=tr // C),
        grid=(batch, H, nr),
        in_specs=[
            pl.BlockSpec((nq, tr, LANE), lambda b, h, i: (h, b * nr + i, 0)),
            pl.BlockSpec((nq, tr, LANE), lambda b, h, i: (kq0 + h, b * nr + i, 0)),
            pl.BlockSpec((nv, tr, LANE), lambda b, h, i: (v0 + h, b * nr + i, 0)),
            pl.BlockSpec((nv, tr, LANE), lambda b, h, i: (g0 + h, b * nr + i, 0)),
            pl.BlockSpec((1, C, C), lambda b, h, i: (h, 0, 0)),
            pl.BlockSpec((1, C, 1), lambda b, h, i: (h, 0, 0)),
            pl.BlockSpec((1, C, 1), lambda b, h, i: (h, 0, 0)),
            pl.BlockSpec((1, 1, 1), lambda b, h, i: (h, 0, 0)),
        ],
        out_specs=pl.BlockSpec((nv, tr, LANE), lambda b, h, i: (h, b * nr + i, 0)),
        out_shape=jax.ShapeDtypeStruct((H * nv, T, LANE), bf16),
        scratch_shapes=[pltpu.VMEM((RET_QK_DIM, RET_V_DIM), f32)],
        compiler_params=pltpu.CompilerParams(
            dimension_semantics=("arbitrary", "arbitrary", "arbitrary")),
        name="retention",
    )(z, z, z, z, intra, q_decay, k_decay, chunk_decay)


def _out_proj_kernel(*refs, n_y, final_norm):
    y_refs = refs[:n_y]
    w_ref, x_ref = refs[n_y], refs[n_y + 1]
    g_ref = refs[n_y + 2] if final_norm else None
    o_ref = refs[-1]
    y = jnp.concatenate([r[c] for r in y_refs for c in range(r.shape[0])], axis=1)
    out = x_ref[...] + jnp.dot(y, w_ref[...], preferred_element_type=f32)
    if final_norm:
        ms = jnp.mean(out * out, axis=-1, keepdims=True)
        out = out * lax.rsqrt(ms + EPS) * g_ref[...]
    o_ref[...] = out


def _out_proj(ys, w, x2d, *, gain=None, tm=512, name):
    T, D = x2d.shape
    tm = min(tm, T)
    in_specs = [pl.BlockSpec((y.shape[0], tm, LANE), lambda i: (0, i, 0)) for y in ys]
    in_specs += [pl.BlockSpec(w.shape, lambda i: (0, 0)),
                 pl.BlockSpec((tm, D), lambda i: (i, 0))]
    args = list(ys) + [w, x2d]
    if gain is not None:
        in_specs.append(pl.BlockSpec((1, D), lambda i: (0, 0)))
        args.append(gain.reshape(1, D).astype(f32))
    return pl.pallas_call(
        functools.partial(_out_proj_kernel, n_y=len(ys), final_norm=gain is not None),
        grid=(T // tm,),
        in_specs=in_specs,
        out_specs=pl.BlockSpec((tm, D), lambda i: (i, 0)),
        out_shape=jax.ShapeDtypeStruct((T, D), f32),
        compiler_params=pltpu.CompilerParams(dimension_semantics=("arbitrary",)),
        name=name,
    )(*args)


def _even_layer(x2d, g_norm, w_in, b_f, w_out, *, batch, seq):
    D = x2d.shape[1]
    a_w = 4 * WIDTH_FOX
    nf = N_HEADS_FOX
    scale = HEAD_DIM ** -0.5
    col_scale = jnp.ones((a_w + 4 * WIDTH_DIL,), f32)
    col_scale = col_scale.at[:WIDTH_FOX].set(scale * LOG2E)
    col_scale = col_scale.at[a_w:a_w + WIDTH_DIL].set(scale * LOG2E)
    w_main = jnp.concatenate([w_in[:, :a_w], w_in[:, a_w + nf:]], axis=1) * col_scale
    w_f = jnp.pad(w_in[:, a_w:a_w + nf], ((0, 0), (0, LANE - nf)))
    z, f2d = _in_proj(x2d, g_norm, w_main.astype(bf16), seq=seq, wf=w_f.astype(bf16),
                      name="even_in_proj")
    b_pad = jnp.pad(b_f.astype(f32), (0, LANE - nf)).reshape(1, LANE)
    qb, kb = _forget_cumsum(f2d, b_pad, batch=batch, seq=seq)
    ya = _fox_attention(z, qb, kb, batch=batch, seq=seq)
    yb = _dilated_attention(z, batch=batch, seq=seq, group0=4 * PAIRS_FOX)
    return _out_proj([ya, yb], w_out.astype(bf16), x2d, name="even_out_proj")


def _odd_layer(x2d, g_norm, w_in, w_out, *, batch, seq, final_gain):
    H, dk = N_HEADS_RET, RET_QK_DIM
    qk_w = H * dk
    half = dk // 2
    perm = (jnp.arange(H)[:, None, None] * dk
            + jnp.arange(2)[None, :, None]
            + 2 * jnp.arange(half)[None, None, :]).reshape(-1)
    w_q = w_in[:, :qk_w][:, perm]
    w_k = w_in[:, qk_w:2 * qk_w][:, perm] * (dk ** -0.5)
    w_all = jnp.concatenate([w_q, w_k, w_in[:, 2 * qk_w:]], axis=1).astype(bf16)

    inv = 1.0 / (ROT_BASE ** jnp.linspace(0.0, 1.0, half, dtype=f32))
    ang = jnp.arange(seq, dtype=f32)[:, None] * inv[None, :]
    rot = (jnp.cos(ang), jnp.sin(ang))
    (z,) = _in_proj(x2d, g_norm, w_all, seq=seq, rot=rot, n_rot_blocks=2 * qk_w // 1024,
                    name="odd_in_proj")
    yc = _retention(z, batch=batch, seq=seq)
    return _out_proj([yc], w_out.astype(bf16), x2d, gain=final_gain, name="odd_out_proj")


def kernel(x, even_norm, even_w_in, even_b_f, even_w_out, odd_norm, odd_w_in, odd_w_out, final_norm):
    batch, seq, D = x.shape
    x2d = x.reshape(batch * seq, D)
    x2d = _even_layer(x2d, even_norm[0], even_w_in[0], even_b_f[0], even_w_out[0],
                      batch=batch, seq=seq)
    out = _odd_layer(x2d, odd_norm[0], odd_w_in[0], odd_w_out[0],
                     batch=batch, seq=seq, final_gain=final_norm)
    return out.reshape(batch, seq, D)
```

```python
import functools

import numpy as np
import jax
import jax.numpy as jnp
from jax import lax
from jax.experimental import pallas as pl
from jax.experimental.pallas import tpu as pltpu

f32 = jnp.float32
bf16 = jnp.bfloat16

LANE = 128
SUBLANE = 8
HEAD_DIM = 64
N_HEADS_FOX = 16
N_HEADS_DIL = 16
WIDTH_FOX = N_HEADS_FOX * HEAD_DIM
WIDTH_DIL = N_HEADS_DIL * HEAD_DIM
DIL_BLOCK = 128
DIL_PATTERNS = ((128, 1), (512, 4), (2048, 16))
DIL_CHUNK = 2048
DIL_DEINTERLEAVE = 4
DIL_UNROLL = 8
N_HEADS_RET = 4
RET_QK_DIM = 256
RET_V_DIM = 512
RET_CHUNK = 256
ROT_BASE = 10000.0
EPS = 1e-6
NEG = -1e30

LOG2E = 1.4426950408889634
N_PIECES = 3
BIAS_LANES = 8
FOX_ROW_BLOCK = 32
FOX_SKIP_LOG2 = 160.0

PAIRS_FOX = WIDTH_FOX // LANE
PAIRS_DIL = WIDTH_DIL // LANE


def _silu(g):
    return g / (1.0 + jnp.exp(-g))


def _in_proj_kernel(*refs, n_groups, with_forget, n_rot_blocks):
    it = iter(refs)
    x_ref, g_ref, w_ref = next(it), next(it), next(it)
    wf_ref = next(it) if with_forget else None
    cos_ref = sin_ref = None
    if n_rot_blocks:
        cos_ref, sin_ref = next(it), next(it)
    o_ref = next(it)
    f_ref = next(it) if with_forget else None
    h_ref = next(it)

    j = pl.program_id(1)

    @pl.when(j == 0)
    def _():
        x = x_ref[...]
        ms = jnp.mean(x * x, axis=-1, keepdims=True)
        h = (x * lax.rsqrt(ms + EPS) * g_ref[...]).astype(bf16)
        h_ref[...] = h
        if with_forget:
            f_ref[...] = jnp.dot(h, wf_ref[...], preferred_element_type=f32)

    acc = jnp.dot(h_ref[...], w_ref[...], preferred_element_type=f32)

    def store_plain():
        for c in range(n_groups):
            o_ref[c] = acc[:, c * LANE:(c + 1) * LANE].astype(bf16)

    if not n_rot_blocks:
        store_plain()
        return

    @pl.when(j < n_rot_blocks)
    def _():
        cos, sin = cos_ref[...], sin_ref[...]
        for c in range(0, n_groups, 2):
            x1 = acc[:, c * LANE:(c + 1) * LANE]
            x2 = acc[:, (c + 1) * LANE:(c + 2) * LANE]
            o_ref[c] = (x1 * cos - x2 * sin).astype(bf16)
            o_ref[c + 1] = (x1 * sin + x2 * cos).astype(bf16)

    @pl.when(j >= n_rot_blocks)
    def _():
        store_plain()


def _in_proj(x2d, gain, w, *, seq, wf=None, rot=None, n_rot_blocks=0, tm=1024, bn=1024, name):
    T, D = x2d.shape
    N = w.shape[1]
    tm = min(tm, seq)
    n_groups = bn // LANE
    grid = (T // tm, N // bn)
    in_specs = [
        pl.BlockSpec((tm, D), lambda i, j: (i, 0)),
        pl.BlockSpec((1, D), lambda i, j: (0, 0)),
        pl.BlockSpec((D, bn), lambda i, j: (0, j)),
    ]
    args = [x2d, gain.reshape(1, D).astype(f32), w]
    out_shape = [jax.ShapeDtypeStruct((N // LANE, T, LANE), bf16)]
    out_specs = [pl.BlockSpec((n_groups, tm, LANE), lambda i, j: (j, i, 0))]
    if wf is not None:
        in_specs.append(pl.BlockSpec((D, LANE), lambda i, j: (0, 0)))
        args.append(wf)
        out_shape.append(jax.ShapeDtypeStruct((T, LANE), f32))
        out_specs.append(pl.BlockSpec((tm, LANE), lambda i, j: (i, 0)))
    if n_rot_blocks:
        spb = seq // tm
        for t in rot:
            in_specs.append(pl.BlockSpec((tm, LANE), lambda i, j: (i % spb, 0)))
            args.append(t)
    kern = functools.partial(_in_proj_kernel, n_groups=n_groups,
                             with_forget=wf is not None, n_rot_blocks=n_rot_blocks)
    outs = pl.pallas_call(
        kern,
        grid=grid,
        in_specs=in_specs,
        out_specs=out_specs,
        out_shape=out_shape,
        scratch_shapes=[pltpu.VMEM((tm, D), bf16)],
        compiler_params=pltpu.CompilerParams(
            dimension_semantics=("arbitrary", "arbitrary")),
        name=name,
    )(*args)
    return outs


def _split_bf16(x):
    p0 = x.astype(bf16)
    r1 = x - p0.astype(f32)
    p1 = r1.astype(bf16)
    p2 = (r1 - p1.astype(f32)).astype(bf16)
    return jnp.concatenate([p0, p1, p2], axis=1)


def _cumsum_kernel(f_ref, b_ref, pq_ref, oq_ref, pk_ref, ok_ref, qb_ref, kb_ref, c2_ref,
                   carry_ref, *, tc):
    @pl.when(pl.program_id(1) == 0)
    def _():
        carry_ref[...] = jnp.zeros_like(carry_ref)

    z = f_ref[...] + b_ref[...]
    lf = jnp.minimum(z, 0.0) - jnp.log1p(jnp.exp(-jnp.abs(z)))
    row = lax.broadcasted_iota(jnp.int32, (tc, tc), 0)
    col = lax.broadcasted_iota(jnp.int32, (tc, tc), 1)
    tri = jnp.where(col <= row, 1.0, 0.0).astype(bf16)
    cs = jnp.dot(tri, _split_bf16(lf), preferred_element_type=f32)
    c = (cs[:, :LANE] + cs[:, LANE:2 * LANE] + cs[:, 2 * LANE:]) + carry_ref[...]
    carry_ref[...] = c[tc - 1:tc, :]

    c2 = c * LOG2E
    c2_ref[...] = c2
    pieces = _split_bf16(c2)
    qb = jnp.dot(pieces, pq_ref[...], preferred_element_type=f32) + oq_ref[...]
    kb = jnp.dot(pieces, pk_ref[...], preferred_element_type=f32) + ok_ref[...]
    qb_ref[...] = qb.astype(bf16)
    kb_ref[...] = kb.astype(bf16)


def _bias_placement():
    assert N_HEADS_FOX * BIAS_LANES == LANE and 2 * N_PIECES <= BIAS_LANES
    pq = np.zeros((N_PIECES * LANE, LANE), np.float32)
    pk = np.zeros((N_PIECES * LANE, LANE), np.float32)
    oq = np.zeros((1, LANE), np.float32)
    ok = np.zeros((1, LANE), np.float32)
    for head in range(N_HEADS_FOX):
        base = BIAS_LANES * head
        for piece in range(N_PIECES):
            pq[piece * LANE + head, base + piece] = 1.0
            ok[0, base + piece] = 1.0
            oq[0, base + N_PIECES + piece] = 1.0
            pk[piece * LANE + head, base + N_PIECES + piece] = -1.0
    return (jnp.asarray(pq, bf16), jnp.asarray(oq, f32),
            jnp.asarray(pk, bf16), jnp.asarray(ok, f32))


def _forget_cumsum(f2d, b_pad, *, batch, seq, tc=512):
    tc = min(tc, seq)
    nb = seq // tc
    T = batch * seq
    pq, oq, pk, ok = _bias_placement()
    full = lambda a: pl.BlockSpec(a.shape, lambda b, i: (0, 0))
    row_blk = pl.BlockSpec((tc, LANE), lambda b, i: (b * nb + i, 0))
    return pl.pallas_call(
        functools.partial(_cumsum_kernel, tc=tc),
        grid=(batch, nb),
        in_specs=[row_blk, full(b_pad), full(pq), full(oq), full(pk), full(ok)],
        out_specs=[row_blk, row_blk, row_blk],
        out_shape=[jax.ShapeDtypeStruct((T, LANE), bf16)] * 2
        + [jax.ShapeDtypeStruct((T, LANE), f32)],
        scratch_shapes=[pltpu.VMEM((1, LANE), f32)],
        compiler_params=pltpu.CompilerParams(
            dimension_semantics=("arbitrary", "arbitrary")),
        name="forget_cumsum",
    )(f2d, b_pad, pq, oq, pk, ok)


def _fox_kernel(q_ref, k_ref, v_ref, g_ref, qb_ref, kb_ref, cs_ref, ce_ref, o_ref,
                s_ref, p_ref, alpha_ref, m_ref, acc_ref, j0_ref, *, tq, n_blocks):
    qi = pl.program_id(2)
    lane = lax.broadcasted_iota(jnp.int32, (1, LANE), 1)
    lo = lane < HEAD_DIM
    own = (lo, jnp.logical_not(lo))
    head0 = 2 * pl.program_id(1)
    own_bias = [(lane >= (head0 + h) * BIAS_LANES) & (lane < (head0 + h + 1) * BIAS_LANES)
                for h in range(2)]
    q, qb = q_ref[0, pl.ds(pl.multiple_of(qi * tq, tq), tq), :], qb_ref[...]
    zero = jnp.zeros_like(q)
    q2 = [jnp.concatenate([jnp.where(own[h], q, zero), jnp.where(own_bias[h], qb, zero)],
                          axis=1) for h in range(2)]
    rb = FOX_ROW_BLOCK
    row_in_blk = lax.broadcasted_iota(jnp.int32, (rb, LANE), 0)
    col = lax.broadcasted_iota(jnp.int32, (rb, LANE), 1)

    sel_row = lax.broadcasted_iota(jnp.int32, (2 * LANE, LANE), 0) % LANE
    sel_col = lax.broadcasted_iota(jnp.int32, (2 * LANE, LANE), 1)
    sel = jnp.where(sel_col == (sel_row >= HEAD_DIM).astype(jnp.int32), 1.0, 0.0).astype(bf16)

    def max_sq_norm(x):
        xf = x.astype(f32)
        sq = xf * xf
        hi = sq.astype(bf16)
        lo = (sq - hi.astype(f32)).astype(bf16)
        n2 = jnp.dot(jnp.concatenate([hi, lo], axis=1), sel, preferred_element_type=f32)
        return jnp.max(n2, axis=0, keepdims=True)

    @pl.when(qi == 0)
    def _():
        blocks = [slice(j * tq, (j + 1) * tq) for j in range(n_blocks)]
        k_max = functools.reduce(jnp.maximum, [max_sq_norm(k_ref[0, b, :]) for b in blocks])
        q_max = jnp.concatenate([max_sq_norm(q_ref[0, b, :]) for b in blocks], axis=0)
        qk = 2.0 * jnp.sqrt(q_max * k_max)
        eye = (lax.broadcasted_iota(jnp.int32, (n_blocks, n_blocks), 0)
               == lax.broadcasted_iota(jnp.int32, (n_blocks, n_blocks), 1))
        skip = (lax.broadcasted_iota(jnp.int32, (n_blocks, n_blocks), 1)
                < lax.broadcasted_iota(jnp.int32, (n_blocks, n_blocks), 0))
        for h in range(2):
            on_head = lane == head0 + h
            col_of = lambda x, m: jnp.sum(jnp.where(m, x, 0.0), axis=1, keepdims=True)
            hi_i = col_of(qk, lane == h) + col_of(cs_ref[0], on_head)
            ce_j = jnp.sum(jnp.where(eye, col_of(ce_ref[0], on_head), 0.0),
                           axis=0, keepdims=True)
            skip = skip & (hi_i - ce_j < -FOX_SKIP_LOG2)
        n_skip = jnp.sum(jnp.where(skip, 1.0, 0.0), axis=1, keepdims=True)
        for i in range(n_blocks):
            j0_ref[i] = n_skip[i, 0].astype(jnp.int32)

    j0 = j0_ref[qi]

    m_ref[...] = jnp.full(m_ref.shape, NEG, f32)
    acc_ref[...] = jnp.zeros(acc_ref.shape, f32)

    def scores(j, slot):
        r0 = pl.multiple_of(j * tq, tq)
        k2 = jnp.concatenate([k_ref[0, pl.ds(r0, tq), :], kb_ref[pl.ds(r0, tq), :]],
                             axis=1)
        for h in range(2):
            s_ref[slot, h] = lax.dot_general(q2[h], k2, (((1,), (1,)), ((), ())),
                                             preferred_element_type=f32)

    def softmax(slot, masked):
        for h in range(2):
            for r in range(tq // rb):
                rows = pl.ds(r * rb, rb)
                tiles = []
                for t in range(tq // LANE):
                    s = s_ref[slot, h, rows, t * LANE:(t + 1) * LANE]
                    if masked:
                        s = jnp.where(col + t * LANE <= row_in_blk + r * rb, s, NEG)
                    tiles.append(s)
                mx = functools.reduce(jnp.maximum, tiles)
                m_old = m_ref[h, rows, :]
                m_new = jnp.maximum(m_old, jnp.max(mx, axis=-1, keepdims=True))
                m_ref[h, rows, :] = m_new
                alpha_ref[slot, h, rows, :] = jnp.exp2(m_old - m_new)
                for t, s in enumerate(tiles):
                    p_ref[slot, h, rows, t * LANE:(t + 1) * LANE] = (
                        jnp.exp2(s - m_new).astype(bf16))

    def accumulate(j, slot):
        r0 = pl.multiple_of(j * tq, tq)
        v = v_ref[0, pl.ds(r0, tq), :]
        for h in range(2):
            vh = jnp.where(own[h], v, jnp.ones_like(v))
            acc_ref[h] = alpha_ref[slot, h] * acc_ref[h] + jnp.dot(
                p_ref[slot, h], vh, preferred_element_type=f32)

    def step_pair(jj, _):
        j = j0 + 2 * jj
        scores(j, 0)
        scores(j + 1, 1)
        softmax(0, False)
        accumulate(j, 0)
        softmax(1, False)
        accumulate(j + 1, 1)
        return 0

    def step_single(j, masked):
        scores(j, 0)
        softmax(0, masked)
        accumulate(j, 0)
        return 0

    n_live = qi - j0
    lax.fori_loop(0, n_live // 2, step_pair, 0)
    lax.fori_loop(0, n_live % 2, lambda _, c: step_single(qi - 1, False), 0)
    step_single(qi, True)

    acc0, acc1 = acc_ref[0], acc_ref[1]
    num = jnp.where(lo, acc0, acc1)
    den = pltpu.roll(jnp.where(lo, acc1, acc0), HEAD_DIM, axis=1)
    o_ref[0] = (num / den * _silu(g_ref[0].astype(f32))).astype(bf16)


def _fox_attention(z, qb, kb, c2, *, batch, seq, tq=512):
    tq = min(tq, seq)
    nq = seq // tq
    P = PAIRS_FOX
    T = batch * seq
    c_blocks = c2.reshape(batch, nq, tq, LANE)
    c_first, c_last = c_blocks[:, :, 0, :], c_blocks[:, :, tq - 1, :]
    q_blk = lambda g0: pl.BlockSpec((1, tq, LANE), lambda b, p, i: (g0 + p, b * nq + i, 0))
    kv_blk = lambda g0: pl.BlockSpec((1, seq, LANE), lambda b, p, i: (g0 + p, b, 0))
    c_blk = pl.BlockSpec((1, nq, LANE), lambda b, p, i: (b, 0, 0))
    return pl.pallas_call(
        functools.partial(_fox_kernel, tq=tq, n_blocks=nq),
        grid=(batch, P, nq),
        in_specs=[kv_blk(0), kv_blk(P), kv_blk(2 * P), q_blk(3 * P),
                  pl.BlockSpec((tq, LANE), lambda b, p, i: (b * nq + i, 0)),
                  pl.BlockSpec((seq, LANE), lambda b, p, i: (b, 0)),
                  c_blk, c_blk],
        out_specs=q_blk(0),
        out_shape=jax.ShapeDtypeStruct((P, T, LANE), bf16),
        scratch_shapes=[
            pltpu.VMEM((2, 2, tq, tq), f32),
            pltpu.VMEM((2, 2, tq, tq), bf16),
            pltpu.VMEM((2, 2, tq, LANE), f32),
            pltpu.VMEM((2, tq, LANE), f32),
            pltpu.VMEM((2, tq, LANE), f32),
            pltpu.SMEM((nq,), jnp.int32),
        ],
        compiler_params=pltpu.CompilerParams(
            dimension_semantics=("arbitrary", "arbitrary", "arbitrary")),
        name="fox_attention",
    )(z, z, z, z, qb, kb, c_first, c_last)


def _dil_kernel(q_ref, kp_ref, kc_ref, vp_ref, vc_ref, g_ref, o_ref,
                qs_ref, ks_ref, vs_ref, q4_ref, k4_ref, v4_ref, os_ref, ls_ref):
    chunk = pl.program_id(2)
    CH, BLK, DEI = DIL_CHUNK, DIL_BLOCK, DIL_DEINTERLEAVE
    qs_ref[...] = q_ref[0].astype(f32)
    ks_ref[0:CH, :] = kp_ref[0].astype(f32)
    ks_ref[CH:2 * CH, :] = kc_ref[0].astype(f32)
    vs_ref[0:CH, :] = vp_ref[0].astype(f32)
    vs_ref[CH:2 * CH, :] = vc_ref[0].astype(f32)
    for res in range(DEI):
        nq, nk = CH // DEI, 2 * CH // DEI
        q4_ref[res * nq:(res + 1) * nq, :] = qs_ref[pl.ds(res, nq, stride=DEI), :]
        k4_ref[res * nk:(res + 1) * nk, :] = ks_ref[pl.ds(res, nk, stride=DEI), :]
        v4_ref[res * nk:(res + 1) * nk, :] = vs_ref[pl.ds(res, nk, stride=DEI), :]

    lane = lax.broadcasted_iota(jnp.int32, (1, LANE), 1)
    lo = lane < HEAD_DIM
    hi = jnp.logical_not(lo)
    qi = lax.broadcasted_iota(jnp.int32, (BLK, 2 * BLK), 0)
    kj = lax.broadcasted_iota(jnp.int32, (BLK, 2 * BLK), 1)
    dist = BLK + qi - kj
    n_tiles = CH // BLK

    for p, (window, r) in enumerate(DIL_PATTERNS):
        n_keys = window // r
        band = (dist >= 0) & (dist <= n_keys)
        bias_band = jnp.where(band, 0.0, NEG)
        bias_first = jnp.where(band & (kj >= BLK), 0.0, NEG)
        span = r * BLK

        def tile(t, r=r, span=span, p=p, bias_band=bias_band, bias_first=bias_first):
            sidx = t // r
            qstart = sidx * span + t % r
            kstart = CH + qstart - span
            no_prev = jnp.logical_and(chunk == 0, sidx == 0)
            bias = jnp.where(no_prev, bias_first, bias_band)
            if r % DEI == 0:
                rr = r // DEI
                res4, sub = (t % r) % DEI, (t % r) // DEI
                q0 = res4 * (CH // DEI) + sidx * (span // DEI) + sub
                k0 = res4 * (2 * CH // DEI) + CH // DEI + sidx * (span // DEI) + sub - span // DEI
                stride = rr if rr > 1 else None
                qt = q4_ref[pl.ds(q0, BLK, stride=stride), :].astype(bf16)
                kt = k4_ref[pl.ds(k0, 2 * BLK, stride=stride), :].astype(bf16)
                vt = v4_ref[pl.ds(k0, 2 * BLK, stride=stride), :].astype(bf16)
            else:
                qt = qs_ref[pl.ds(qstart, BLK, stride=r), :].astype(bf16)
                kt = ks_ref[pl.ds(kstart, 2 * BLK, stride=r), :].astype(bf16)
                vt = vs_ref[pl.ds(kstart, 2 * BLK, stride=r), :].astype(bf16)
            res, ms = [], []
            for own in (lo, hi):
                qh = jnp.where(own, qt, jnp.zeros_like(qt))
                vh = jnp.where(own, vt, jnp.ones_like(vt))
                s = lax.dot_general(qh, kt, (((1,), (1,)), ((), ())),
                                    preferred_element_type=f32) + bias
                m = jnp.max(s, axis=-1, keepdims=True)
                e = jnp.exp2(s - m).astype(bf16)
                res.append(jnp.dot(e, vh, preferred_element_type=f32))
                ms.append(m)
            num = jnp.where(lo, res[0], res[1])
            den = pltpu.roll(jnp.where(lo, res[1], res[0]), HEAD_DIM, axis=1)
            os_ref[p, pl.ds(qstart, BLK, stride=r), :] = num / den
            ls_ref[p, pl.ds(qstart, BLK, stride=r), :] = (
                jnp.where(lo, ms[0], ms[1]) + jnp.log2(den))

        def group(g, _, tile=tile):
            for u in range(DIL_UNROLL):
                tile(g * DIL_UNROLL + u)
            return 0

        lax.fori_loop(0, n_tiles // DIL_UNROLL, group, 0)

    l0, l1, l2 = ls_ref[0], ls_ref[1], ls_ref[2]
    mx = jnp.maximum(jnp.maximum(l0, l1), l2)
    w0, w1, w2 = jnp.exp2(l0 - mx), jnp.exp2(l1 - mx), jnp.exp2(l2 - mx)
    y = (w0 * os_ref[0] + w1 * os_ref[1] + w2 * os_ref[2]) / (w0 + w1 + w2)
    o_ref[0] = (y * _silu(g_ref[0].astype(f32))).astype(bf16)


def _dilated_attention(z, *, batch, seq, group0):
    CH = DIL_CHUNK
    assert seq % CH == 0
    nc = seq // CH
    P = PAIRS_DIL
    T = batch * seq
    g0 = group0

    def cur(off):
        return lambda b, p, c: (g0 + off * P + p, b * nc + c, 0)

    def prev(off):
        return lambda b, p, c: (g0 + off * P + p, b * nc + jnp.maximum(c - 1, 0), 0)

    blk = (1, CH, LANE)
    return pl.pallas_call(
        _dil_kernel,
        grid=(batch, P, nc),
        in_specs=[
            pl.BlockSpec(blk, cur(0)),
            pl.BlockSpec(blk, prev(1)), pl.BlockSpec(blk, cur(1)),
            pl.BlockSpec(blk, prev(2)), pl.BlockSpec(blk, cur(2)),
            pl.BlockSpec(blk, cur(3)),
        ],
        out_specs=pl.BlockSpec(blk, lambda b, p, c: (p, b * nc + c, 0)),
        out_shape=jax.ShapeDtypeStruct((P, T, LANE), bf16),
        scratch_shapes=[
            pltpu.VMEM((CH, LANE), f32),
            pltpu.VMEM((2 * CH, LANE), f32),
            pltpu.VMEM((2 * CH, LANE), f32),
            pltpu.VMEM((CH, LANE), f32),
            pltpu.VMEM((2 * CH, LANE), f32),
            pltpu.VMEM((2 * CH, LANE), f32),
            pltpu.VMEM((len(DIL_PATTERNS), CH, LANE), f32),
            pltpu.VMEM((len(DIL_PATTERNS), CH, LANE), f32),
        ],
        compiler_params=pltpu.CompilerParams(
            dimension_semantics=("arbitrary", "arbitrary", "arbitrary")),
        name="dilated_attention",
    )(z, z, z, z, z, z)


def _ret_kernel(q_ref, k_ref, v_ref, g_ref, intra_ref, qd_ref, kd_ref, cd_ref,
                o_ref, state_ref, *, C, n_chunks):
    @pl.when(pl.program_id(2) == 0)
    def _():
        state_ref[...] = jnp.zeros_like(state_ref)

    nq = RET_QK_DIM // LANE
    nv = RET_V_DIM // LANE
    intra = intra_ref[0]
    qd = qd_ref[0]
    kd = kd_ref[0]
    cd = cd_ref[0]

    st = state_ref[...]
    for ci in range(n_chunks):
        r0 = ci * C
        q = jnp.concatenate([q_ref[c, pl.ds(r0, C), :] for c in range(nq)], axis=1)
        k = jnp.concatenate([k_ref[c, pl.ds(r0, C), :] for c in range(nq)], axis=1)
        v = jnp.concatenate([v_ref[c, pl.ds(r0, C), :] for c in range(nv)], axis=1)
        s = lax.dot_general(q, k, (((1,), (1,)), ((), ())),
                            preferred_element_type=f32) * intra
        inner = jnp.dot(s.astype(bf16), v, preferred_element_type=f32)
        cross = jnp.dot(q, st.astype(bf16), preferred_element_type=f32) * qd
        kdec = (k.astype(f32) * kd).astype(bf16)
        upd = lax.dot_general(kdec, v, (((0,), (0,)), ((), ())),
                              preferred_element_type=f32)
        st = st * cd + upd
        if ci == n_chunks - 1:
            state_ref[...] = st
        y = inner + cross
        mu = jnp.mean(y, axis=-1, keepdims=True)
        yc = y - mu
        var = jnp.mean(yc * yc, axis=-1, keepdims=True)
        yn = yc * lax.rsqrt(var + EPS)
        g = jnp.concatenate([g_ref[c, pl.ds(r0, C), :] for c in range(nv)],
                            axis=1).astype(f32)
        out = (yn * _silu(g)).astype(bf16)
        for c in range(nv):
            o_ref[c, pl.ds(r0, C), :] = out[:, c * LANE:(c + 1) * LANE]


def _retention(z, *, batch, seq, C=RET_CHUNK, tr=1024):
    tr = min(tr, seq)
    nr = seq // tr
    H = N_HEADS_RET
    T = batch * seq
    nq = RET_QK_DIM // LANE
    nv = RET_V_DIM // LANE
    kq0 = H
    v0 = 2 * H * nq // nv
    g0 = v0 + H

    log_gamma = jnp.log1p(-jnp.power(2.0, -5.0 - jnp.arange(H, dtype=f32)))
    pos = jnp.arange(C, dtype=f32)
    rel = pos[:, None] - pos[None, :]
    intra = jnp.where(rel >= 0,
                      jnp.exp(log_gamma[:, None, None] * jnp.maximum(rel, 0.0)), 0.0)
    q_decay = jnp.exp(log_gamma[:, None] * (pos + 1.0))[:, :, None]
    k_decay = jnp.exp(log_gamma[:, None] * (C - 1.0 - pos))[:, :, None]
    chunk_decay = jnp.exp(log_gamma * C)[:, None, None]

    return pl.pallas_call(
        functools.partial(_ret_kernel, C=C, n_chunks=tr // C),
        grid=(batch, H, nr),
        in_specs=[
            pl.BlockSpec((nq, tr, LANE), lambda b, h, i: (h, b * nr + i, 0)),
            pl.BlockSpec((nq, tr, LANE), lambda b, h, i: (kq0 + h, b * nr + i, 0)),
            pl.BlockSpec((nv, tr, LANE), lambda b, h, i: (v0 + h, b * nr + i, 0)),
            pl.BlockSpec((nv, tr, LANE), lambda b, h, i: (g0 + h, b * nr + i, 0)),
            pl.BlockSpec((1, C, C), lambda b, h, i: (h, 0, 0)),
            pl.BlockSpec((1, C, 1), lambda b, h, i: (h, 0, 0)),
            pl.BlockSpec((1, C, 1), lambda b, h, i: (h, 0, 0)),
            pl.BlockSpec((1, 1, 1), lambda b, h, i: (h, 0, 0)),
        ],
        out_specs=pl.BlockSpec((nv, tr, LANE), lambda b, h, i: (h, b * nr + i, 0)),
        out_shape=jax.ShapeDtypeStruct((H * nv, T, LANE), bf16),
        scratch_shapes=[pltpu.VMEM((RET_QK_DIM, RET_V_DIM), f32)],
        compiler_params=pltpu.CompilerParams(
            dimension_semantics=("arbitrary", "arbitrary", "arbitrary")),
        name="retention",
    )(z, z, z, z, intra, q_decay, k_decay, chunk_decay)


def _out_proj_kernel(*refs, n_y, final_norm):
    y_refs = refs[:n_y]
    w_ref, x_ref = refs[n_y], refs[n_y + 1]
    g_ref = refs[n_y + 2] if final_norm else None
    o_ref = refs[-1]
    y = jnp.concatenate([r[c] for r in y_refs for c in range(r.shape[0])], axis=1)
    out = x_ref[...] + jnp.dot(y, w_ref[...], preferred_element_type=f32)
    if final_norm:
        ms = jnp.mean(out * out, axis=-1, keepdims=True)
        out = out * lax.rsqrt(ms + EPS) * g_ref[...]
    o_ref[...] = out


def _out_proj(ys, w, x2d, *, gain=None, tm=512, name):
    T, D = x2d.shape
    tm = min(tm, T)
    in_specs = [pl.BlockSpec((y.shape[0], tm, LANE), lambda i: (0, i, 0)) for y in ys]
    in_specs += [pl.BlockSpec(w.shape, lambda i: (0, 0)),
                 pl.BlockSpec((tm, D), lambda i: (i, 0))]
    args = list(ys) + [w, x2d]
    if gain is not None:
        in_specs.append(pl.BlockSpec((1, D), lambda i: (0, 0)))
        args.append(gain.reshape(1, D).astype(f32))
    return pl.pallas_call(
        functools.partial(_out_proj_kernel, n_y=len(ys), final_norm=gain is not None),
        grid=(T // tm,),
        in_specs=in_specs,
        out_specs=pl.BlockSpec((tm, D), lambda i: (i, 0)),
        out_shape=jax.ShapeDtypeStruct((T, D), f32),
        compiler_params=pltpu.CompilerParams(dimension_semantics=("arbitrary",)),
        name=name,
    )(*args)


def _even_layer(x2d, g_norm, w_in, b_f, w_out, *, batch, seq):
    D = x2d.shape[1]
    a_w = 4 * WIDTH_FOX
    nf = N_HEADS_FOX
    scale = HEAD_DIM ** -0.5
    col_scale = jnp.ones((a_w + 4 * WIDTH_DIL,), f32)
    col_scale = col_scale.at[:WIDTH_FOX].set(scale * LOG2E)
    col_scale = col_scale.at[a_w:a_w + WIDTH_DIL].set(scale * LOG2E)
    w_main = jnp.concatenate([w_in[:, :a_w], w_in[:, a_w + nf:]], axis=1) * col_scale
    w_f = jnp.pad(w_in[:, a_w:a_w + nf], ((0, 0), (0, LANE - nf)))
    z, f2d = _in_proj(x2d, g_norm, w_main.astype(bf16), seq=seq, wf=w_f.astype(bf16),
                      name="even_in_proj")
    b_pad = jnp.pad(b_f.astype(f32), (0, LANE - nf)).reshape(1, LANE)
    qb, kb, c2 = _forget_cumsum(f2d, b_pad, batch=batch, seq=seq)
    ya = _fox_attention(z, qb, kb, c2, batch=batch, seq=seq)
    yb = _dilated_attention(z, batch=batch, seq=seq, group0=4 * PAIRS_FOX)
    return _out_proj([ya, yb], w_out.astype(bf16), x2d, name="even_out_proj")


def _odd_layer(x2d, g_norm, w_in, w_out, *, batch, seq, final_gain):
    H, dk = N_HEADS_RET, RET_QK_DIM
    qk_w = H * dk
    half = dk // 2
    perm = (jnp.arange(H)[:, None, None] * dk
            + jnp.arange(2)[None, :, None]
            + 2 * jnp.arange(half)[None, None, :]).reshape(-1)
    w_q = w_in[:, :qk_w][:, perm]
    w_k = w_in[:, qk_w:2 * qk_w][:, perm] * (dk ** -0.5)
    w_all = jnp.concatenate([w_q, w_k, w_in[:, 2 * qk_w:]], axis=1).astype(bf16)

    inv = 1.0 / (ROT_BASE ** jnp.linspace(0.0, 1.0, half, dtype=f32))
    ang = jnp.arange(seq, dtype=f32)[:, None] * inv[None, :]
    rot = (jnp.cos(ang), jnp.sin(ang))
    (z,) = _in_proj(x2d, g_norm, w_all, seq=seq, rot=rot, n_rot_blocks=2 * qk_w // 1024,
                    name="odd_in_proj")
    yc = _retention(z, batch=batch, seq=seq)
    return _out_proj([yc], w_out.astype(bf16), x2d, gain=final_gain, name="odd_out_proj")


def kernel(x, even_norm, even_w_in, even_b_f, even_w_out, odd_norm, odd_w_in, odd_w_out, final_norm):
    batch, seq, D = x.shape
    x2d = x.reshape(batch * seq, D)
    x2d = _even_layer(x2d, even_norm[0], even_w_in[0], even_b_f[0], even_w_out[0],
                      batch=batch, seq=seq)
    out = _odd_layer(x2d, odd_norm[0], odd_w_in[0], odd_w_out[0],
                     batch=batch, seq=seq, final_gain=final_norm)
    return out.reshape(batch, seq, D)
```

```python
import functools

import numpy as np
import jax
import jax.numpy as jnp
from jax import lax
from jax.experimental import pallas as pl
from jax.experimental.pallas import tpu as pltpu

f32 = jnp.float32
bf16 = jnp.bfloat16

LANE = 128
HEAD_DIM = 64
N_HEADS_FOX = 16
N_HEADS_DIL = 16
WIDTH_FOX = N_HEADS_FOX * HEAD_DIM
WIDTH_DIL = N_HEADS_DIL * HEAD_DIM
DIL_BLOCK = 128
DIL_PATTERNS = ((128, 1), (512, 4), (2048, 16))
DIL_CHUNK = 2048
DIL_DEINTERLEAVE = 4
DIL_UNROLL = 8
N_HEADS_RET = 4
RET_QK_DIM = 256
RET_V_DIM = 512
RET_CHUNK = 256
ROT_BASE = 10000.0
EPS = 1e-6
NEG = -1e30

LOG2E = 1.4426950408889634
N_PIECES = 3
BIAS_LANES = 8
FOX_ROW_BLOCK = 32
FOX_SKIP_LOG2 = 160.0

PAIRS_FOX = WIDTH_FOX // LANE
PAIRS_DIL = WIDTH_DIL // LANE


def _silu(g):
    return g / (1.0 + jnp.exp(-g))


def _in_proj_kernel(*refs, n_groups, with_forget, n_rot_blocks):
    it = iter(refs)
    x_ref, g_ref, w_ref = next(it), next(it), next(it)
    wf_ref = next(it) if with_forget else None
    cos_ref = sin_ref = None
    if n_rot_blocks:
        cos_ref, sin_ref = next(it), next(it)
    o_ref = next(it)
    f_ref = next(it) if with_forget else None
    h_ref = next(it)

    j = pl.program_id(1)

    @pl.when(j == 0)
    def _():
        x = x_ref[...]
        ms = jnp.mean(x * x, axis=-1, keepdims=True)
        h = (x * lax.rsqrt(ms + EPS) * g_ref[...]).astype(bf16)
        h_ref[...] = h
        if with_forget:
            f_ref[...] = jnp.dot(h, wf_ref[...], preferred_element_type=f32)

    acc = jnp.dot(h_ref[...], w_ref[...], preferred_element_type=f32)

    def store_plain():
        for c in range(n_groups):
            o_ref[c] = acc[:, c * LANE:(c + 1) * LANE].astype(bf16)

    if not n_rot_blocks:
        store_plain()
        return

    @pl.when(j < n_rot_blocks)
    def _():
        cos, sin = cos_ref[...], sin_ref[...]
        for c in range(0, n_groups, 2):
            x1 = acc[:, c * LANE:(c + 1) * LANE]
            x2 = acc[:, (c + 1) * LANE:(c + 2) * LANE]
            o_ref[c] = (x1 * cos - x2 * sin).astype(bf16)
            o_ref[c + 1] = (x1 * sin + x2 * cos).astype(bf16)

    @pl.when(j >= n_rot_blocks)
    def _():
        store_plain()


def _in_proj(x2d, gain, w, *, seq, wf=None, rot=None, n_rot_blocks=0, tm=1024, bn=1024, name):
    T, D = x2d.shape
    N = w.shape[1]
    tm = min(tm, seq)
    n_groups = bn // LANE
    grid = (T // tm, N // bn)
    in_specs = [
        pl.BlockSpec((tm, D), lambda i, j: (i, 0)),
        pl.BlockSpec((1, D), lambda i, j: (0, 0)),
        pl.BlockSpec((D, bn), lambda i, j: (0, j)),
    ]
    args = [x2d, gain.reshape(1, D).astype(f32), w]
    out_shape = [jax.ShapeDtypeStruct((N // LANE, T, LANE), bf16)]
    out_specs = [pl.BlockSpec((n_groups, tm, LANE), lambda i, j: (j, i, 0))]
    if wf is not None:
        in_specs.append(pl.BlockSpec((D, LANE), lambda i, j: (0, 0)))
        args.append(wf)
        out_shape.append(jax.ShapeDtypeStruct((T, LANE), f32))
        out_specs.append(pl.BlockSpec((tm, LANE), lambda i, j: (i, 0)))
    if n_rot_blocks:
        spb = seq // tm
        for t in rot:
            in_specs.append(pl.BlockSpec((tm, LANE), lambda i, j: (i % spb, 0)))
            args.append(t)
    kern = functools.partial(_in_proj_kernel, n_groups=n_groups,
                             with_forget=wf is not None, n_rot_blocks=n_rot_blocks)
    outs = pl.pallas_call(
        kern,
        grid=grid,
        in_specs=in_specs,
        out_specs=out_specs,
        out_shape=out_shape,
        scratch_shapes=[pltpu.VMEM((tm, D), bf16)],
        compiler_params=pltpu.CompilerParams(
            dimension_semantics=("arbitrary", "arbitrary")),
        name=name,
    )(*args)
    return outs


def _split_bf16(x):
    p0 = x.astype(bf16)
    r1 = x - p0.astype(f32)
    p1 = r1.astype(bf16)
    p2 = (r1 - p1.astype(f32)).astype(bf16)
    return jnp.concatenate([p0, p1, p2], axis=1)


def _cumsum_kernel(f_ref, b_ref, pq_ref, oq_ref, pk_ref, ok_ref, qb_ref, kb_ref, c2_ref,
                   carry_ref, *, tc):
    @pl.when(pl.program_id(1) == 0)
    def _():
        carry_ref[...] = jnp.zeros_like(carry_ref)

    z = f_ref[...] + b_ref[...]
    lf = jnp.minimum(z, 0.0) - jnp.log1p(jnp.exp(-jnp.abs(z)))
    row = lax.broadcasted_iota(jnp.int32, (tc, tc), 0)
    col = lax.broadcasted_iota(jnp.int32, (tc, tc), 1)
    tri = jnp.where(col <= row, 1.0, 0.0).astype(bf16)
    cs = jnp.dot(tri, _split_bf16(lf), preferred_element_type=f32)
    c = (cs[:, :LANE] + cs[:, LANE:2 * LANE] + cs[:, 2 * LANE:]) + carry_ref[...]
    carry_ref[...] = c[tc - 1:tc, :]

    c2 = c * LOG2E
    c2_ref[...] = c2
    pieces = _split_bf16(c2)
    qb = jnp.dot(pieces, pq_ref[...], preferred_element_type=f32) + oq_ref[...]
    kb = jnp.dot(pieces, pk_ref[...], preferred_element_type=f32) + ok_ref[...]
    qb_ref[...] = qb.astype(bf16)
    kb_ref[...] = kb.astype(bf16)


def _bias_placement():
    assert N_HEADS_FOX * BIAS_LANES == LANE and 2 * N_PIECES <= BIAS_LANES
    pq = np.zeros((N_PIECES * LANE, LANE), np.float32)
    pk = np.zeros((N_PIECES * LANE, LANE), np.float32)
    oq = np.zeros((1, LANE), np.float32)
    ok = np.zeros((1, LANE), np.float32)
    for head in range(N_HEADS_FOX):
        base = BIAS_LANES * head
        for piece in range(N_PIECES):
            pq[piece * LANE + head, base + piece] = 1.0
            ok[0, base + piece] = 1.0
            oq[0, base + N_PIECES + piece] = 1.0
            pk[piece * LANE + head, base + N_PIECES + piece] = -1.0
    return (jnp.asarray(pq, bf16), jnp.asarray(oq, f32),
            jnp.asarray(pk, bf16), jnp.asarray(ok, f32))


def _forget_cumsum(f2d, b_pad, *, batch, seq, tc=512):
    tc = min(tc, seq)
    nb = seq // tc
    T = batch * seq
    pq, oq, pk, ok = _bias_placement()
    full = lambda a: pl.BlockSpec(a.shape, lambda b, i: (0, 0))
    row_blk = pl.BlockSpec((tc, LANE), lambda b, i: (b * nb + i, 0))
    return pl.pallas_call(
        functools.partial(_cumsum_kernel, tc=tc),
        grid=(batch, nb),
        in_specs=[row_blk, full(b_pad), full(pq), full(oq), full(pk), full(ok)],
        out_specs=[row_blk, row_blk, row_blk],
        out_shape=[jax.ShapeDtypeStruct((T, LANE), bf16)] * 2
        + [jax.ShapeDtypeStruct((T, LANE), f32)],
        scratch_shapes=[pltpu.VMEM((1, LANE), f32)],
        compiler_params=pltpu.CompilerParams(
            dimension_semantics=("arbitrary", "arbitrary")),
        name="forget_cumsum",
    )(f2d, b_pad, pq, oq, pk, ok)


def _fox_kernel(q_ref, k_ref, v_ref, g_ref, qb_ref, kb_ref, cs_ref, ce_ref, o_ref,
                s_ref, p_ref, alpha_ref, m_ref, acc_ref, j0_ref, *, tq, n_blocks):
    qi = pl.program_id(2)
    lane = lax.broadcasted_iota(jnp.int32, (1, LANE), 1)
    lo = lane < HEAD_DIM
    own = (lo, jnp.logical_not(lo))
    head0 = 2 * pl.program_id(1)
    own_bias = [(lane >= (head0 + h) * BIAS_LANES) & (lane < (head0 + h + 1) * BIAS_LANES)
                for h in range(2)]
    q, qb = q_ref[0, pl.ds(pl.multiple_of(qi * tq, tq), tq), :], qb_ref[...]
    zero = jnp.zeros_like(q)
    q2 = [jnp.concatenate([jnp.where(own[h], q, zero), jnp.where(own_bias[h], qb, zero)],
                          axis=1) for h in range(2)]
    rb = FOX_ROW_BLOCK
    row_in_blk = lax.broadcasted_iota(jnp.int32, (rb, LANE), 0)
    col = lax.broadcasted_iota(jnp.int32, (rb, LANE), 1)

    sel_row = lax.broadcasted_iota(jnp.int32, (2 * LANE, LANE), 0) % LANE
    sel_col = lax.broadcasted_iota(jnp.int32, (2 * LANE, LANE), 1)
    sel = jnp.where(sel_col == (sel_row >= HEAD_DIM).astype(jnp.int32), 1.0, 0.0).astype(bf16)

    def max_sq_norm(x):
        xf = x.astype(f32)
        sq = xf * xf
        hi = sq.astype(bf16)
        lo = (sq - hi.astype(f32)).astype(bf16)
        n2 = jnp.dot(jnp.concatenate([hi, lo], axis=1), sel, preferred_element_type=f32)
        return jnp.max(n2, axis=0, keepdims=True)

    @pl.when(qi == 0)
    def _():
        blocks = [slice(j * tq, (j + 1) * tq) for j in range(n_blocks)]
        k_max = functools.reduce(jnp.maximum, [max_sq_norm(k_ref[0, b, :]) for b in blocks])
        q_max = jnp.concatenate([max_sq_norm(q_ref[0, b, :]) for b in blocks], axis=0)
        qk = 2.0 * jnp.sqrt(q_max * k_max)
        eye = (lax.broadcasted_iota(jnp.int32, (n_blocks, n_blocks), 0)
               == lax.broadcasted_iota(jnp.int32, (n_blocks, n_blocks), 1))
        earlier = (lax.broadcasted_iota(jnp.int32, (n_blocks, n_blocks), 1)
                   < lax.broadcasted_iota(jnp.int32, (n_blocks, n_blocks), 0))
        for h in range(2):
            on_head = lane == head0 + h
            col_of = lambda x, m: jnp.sum(jnp.where(m, x, 0.0), axis=1, keepdims=True)
            hi_i = col_of(qk, lane == h) + col_of(cs_ref[0], on_head)
            ce_j = jnp.sum(jnp.where(eye, col_of(ce_ref[0], on_head), 0.0),
                           axis=0, keepdims=True)
            skip = earlier & (hi_i - ce_j < -FOX_SKIP_LOG2)
            n_skip = jnp.sum(jnp.where(skip, 1.0, 0.0), axis=1, keepdims=True)
            for i in range(n_blocks):
                j0_ref[h, i] = n_skip[i, 0].astype(jnp.int32)

    m_ref[...] = jnp.full(m_ref.shape, NEG, f32)
    acc_ref[...] = jnp.zeros(acc_ref.shape, f32)

    def scores(j, slot, heads):
        r0 = pl.multiple_of(j * tq, tq)
        k2 = jnp.concatenate([k_ref[0, pl.ds(r0, tq), :], kb_ref[pl.ds(r0, tq), :]],
                             axis=1)
        for h in heads:
            s_ref[slot, h] = lax.dot_general(q2[h], k2, (((1,), (1,)), ((), ())),
                                             preferred_element_type=f32)

    def softmax(slot, masked, heads):
        for h in heads:
            for r in range(tq // rb):
                rows = pl.ds(r * rb, rb)
                tiles = []
                for t in range(tq // LANE):
                    s = s_ref[slot, h, rows, t * LANE:(t + 1) * LANE]
                    if masked:
                        s = jnp.where(col + t * LANE <= row_in_blk + r * rb, s, NEG)
                    tiles.append(s)
                mx = functools.reduce(jnp.maximum, tiles)
                m_old = m_ref[h, rows, :]
                m_new = jnp.maximum(m_old, jnp.max(mx, axis=-1, keepdims=True))
                m_ref[h, rows, :] = m_new
                alpha_ref[slot, h, rows, :] = jnp.exp2(m_old - m_new)
                for t, s in enumerate(tiles):
                    p_ref[slot, h, rows, t * LANE:(t + 1) * LANE] = (
                        jnp.exp2(s - m_new).astype(bf16))

    def accumulate(j, slot, heads):
        r0 = pl.multiple_of(j * tq, tq)
        v = v_ref[0, pl.ds(r0, tq), :]
        for h in heads:
            vh = jnp.where(own[h], v, jnp.ones_like(v))
            acc_ref[h] = alpha_ref[slot, h] * acc_ref[h] + jnp.dot(
                p_ref[slot, h], vh, preferred_element_type=f32)

    def step_single(j, masked, heads):
        scores(j, 0, heads)
        softmax(0, masked, heads)
        accumulate(j, 0, heads)
        return 0

    def run_blocks(j_lo, n, heads):
        def step_pair(jj, _):
            j = j_lo + 2 * jj
            scores(j, 0, heads)
            scores(j + 1, 1, heads)
            softmax(0, False, heads)
            accumulate(j, 0, heads)
            softmax(1, False, heads)
            accumulate(j + 1, 1, heads)
            return 0

        lax.fori_loop(0, n // 2, step_pair, 0)
        lax.fori_loop(0, n % 2, lambda _, c: step_single(j_lo + n - 1, False, heads), 0)

    j0 = [j0_ref[h, qi] for h in range(2)]
    j_both = jnp.maximum(j0[0], j0[1])
    run_blocks(j0[0], j_both - j0[0], (0,))
    run_blocks(j0[1], j_both - j0[1], (1,))
    run_blocks(j_both, qi - j_both, (0, 1))
    step_single(qi, True, (0, 1))

    acc0, acc1 = acc_ref[0], acc_ref[1]
    num = jnp.where(lo, acc0, acc1)
    den = pltpu.roll(jnp.where(lo, acc1, acc0), HEAD_DIM, axis=1)
    o_ref[0] = (num / den * _silu(g_ref[0].astype(f32))).astype(bf16)


def _fox_attention(z, qb, kb, c2, *, batch, seq, tq=512):
    tq = min(tq, seq)
    nq = seq // tq
    P = PAIRS_FOX
    T = batch * seq
    c_blocks = c2.reshape(batch, nq, tq, LANE)
    c_first, c_last = c_blocks[:, :, 0, :], c_blocks[:, :, tq - 1, :]
    q_blk = lambda g0: pl.BlockSpec((1, tq, LANE), lambda b, p, i: (g0 + p, b * nq + i, 0))
    kv_blk = lambda g0: pl.BlockSpec((1, seq, LANE), lambda b, p, i: (g0 + p, b, 0))
    c_blk = pl.BlockSpec((1, nq, LANE), lambda b, p, i: (b, 0, 0))
    return pl.pallas_call(
        functools.partial(_fox_kernel, tq=tq, n_blocks=nq),
        grid=(batch, P, nq),
        in_specs=[kv_blk(0), kv_blk(P), kv_blk(2 * P), q_blk(3 * P),
                  pl.BlockSpec((tq, LANE), lambda b, p, i: (b * nq + i, 0)),
                  pl.BlockSpec((seq, LANE), lambda b, p, i: (b, 0)),
                  c_blk, c_blk],
        out_specs=q_blk(0),
        out_shape=jax.ShapeDtypeStruct((P, T, LANE), bf16),
        scratch_shapes=[
            pltpu.VMEM((2, 2, tq, tq), f32),
            pltpu.VMEM((2, 2, tq, tq), bf16),
            pltpu.VMEM((2, 2, tq, LANE), f32),
            pltpu.VMEM((2, tq, LANE), f32),
            pltpu.VMEM((2, tq, LANE), f32),
            pltpu.SMEM((2, nq), jnp.int32),
        ],
        compiler_params=pltpu.CompilerParams(
            dimension_semantics=("arbitrary", "arbitrary", "arbitrary")),
        name="fox_attention",
    )(z, z, z, z, qb, kb, c_first, c_last)


def _dil_kernel(q_ref, kp_ref, kc_ref, vp_ref, vc_ref, g_ref, o_ref,
                qs_ref, ks_ref, vs_ref, q4_ref, k4_ref, v4_ref, os_ref, ls_ref):
    chunk = pl.program_id(2)
    CH, BLK, DEI = DIL_CHUNK, DIL_BLOCK, DIL_DEINTERLEAVE
    qs_ref[...] = q_ref[0].astype(f32)
    ks_ref[0:CH, :] = kp_ref[0].astype(f32)
    ks_ref[CH:2 * CH, :] = kc_ref[0].astype(f32)
    vs_ref[0:CH, :] = vp_ref[0].astype(f32)
    vs_ref[CH:2 * CH, :] = vc_ref[0].astype(f32)
    for res in range(DEI):
        nq, nk = CH // DEI, 2 * CH // DEI
        q4_ref[res * nq:(res + 1) * nq, :] = qs_ref[pl.ds(res, nq, stride=DEI), :]
        k4_ref[res * nk:(res + 1) * nk, :] = ks_ref[pl.ds(res, nk, stride=DEI), :]
        v4_ref[res * nk:(res + 1) * nk, :] = vs_ref[pl.ds(res, nk, stride=DEI), :]

    lane = lax.broadcasted_iota(jnp.int32, (1, LANE), 1)
    lo = lane < HEAD_DIM
    hi = jnp.logical_not(lo)
    qi = lax.broadcasted_iota(jnp.int32, (BLK, 2 * BLK), 0)
    kj = lax.broadcasted_iota(jnp.int32, (BLK, 2 * BLK), 1)
    dist = BLK + qi - kj
    n_tiles = CH // BLK

    for p, (window, r) in enumerate(DIL_PATTERNS):
        n_keys = window // r
        band = (dist >= 0) & (dist <= n_keys)
        bias_band = jnp.where(band, 0.0, NEG)
        bias_first = jnp.where(band & (kj >= BLK), 0.0, NEG)
        span = r * BLK

        def tile(t, r=r, span=span, p=p, bias_band=bias_band, bias_first=bias_first):
            sidx = t // r
            qstart = sidx * span + t % r
            kstart = CH + qstart - span
            no_prev = jnp.logical_and(chunk == 0, sidx == 0)
            bias = jnp.where(no_prev, bias_first, bias_band)
            if r % DEI == 0:
                rr = r // DEI
                res4, sub = (t % r) % DEI, (t % r) // DEI
                q0 = res4 * (CH // DEI) + sidx * (span // DEI) + sub
                k0 = res4 * (2 * CH // DEI) + CH // DEI + sidx * (span // DEI) + sub - span // DEI
                stride = rr if rr > 1 else None
                qt = q4_ref[pl.ds(q0, BLK, stride=stride), :].astype(bf16)
                kt = k4_ref[pl.ds(k0, 2 * BLK, stride=stride), :].astype(bf16)
                vt = v4_ref[pl.ds(k0, 2 * BLK, stride=stride), :].astype(bf16)
            else:
                qt = qs_ref[pl.ds(qstart, BLK, stride=r), :].astype(bf16)
                kt = ks_ref[pl.ds(kstart, 2 * BLK, stride=r), :].astype(bf16)
                vt = vs_ref[pl.ds(kstart, 2 * BLK, stride=r), :].astype(bf16)
            res, ms = [], []
            for own in (lo, hi):
                qh = jnp.where(own, qt, jnp.zeros_like(qt))
                vh = jnp.where(own, vt, jnp.ones_like(vt))
                s = lax.dot_general(qh, kt, (((1,), (1,)), ((), ())),
                                    preferred_element_type=f32) + bias
                m = jnp.max(s, axis=-1, keepdims=True)
                e = jnp.exp2(s - m).astype(bf16)
                res.append(jnp.dot(e, vh, preferred_element_type=f32))
                ms.append(m)
            num = jnp.where(lo, res[0], res[1])
            den = pltpu.roll(jnp.where(lo, res[1], res[0]), HEAD_DIM, axis=1)
            os_ref[p, pl.ds(qstart, BLK, stride=r), :] = num / den
            ls_ref[p, pl.ds(qstart, BLK, stride=r), :] = (
                jnp.where(lo, ms[0], ms[1]) + jnp.log2(den))

        def group(g, _, tile=tile):
            for u in range(DIL_UNROLL):
                tile(g * DIL_UNROLL + u)
            return 0

        lax.fori_loop(0, n_tiles // DIL_UNROLL, group, 0)

    l0, l1, l2 = ls_ref[0], ls_ref[1], ls_ref[2]
    mx = jnp.maximum(jnp.maximum(l0, l1), l2)
    w0, w1, w2 = jnp.exp2(l0 - mx), jnp.exp2(l1 - mx), jnp.exp2(l2 - mx)
    y = (w0 * os_ref[0] + w1 * os_ref[1] + w2 * os_ref[2]) / (w0 + w1 + w2)
    o_ref[0] = (y * _silu(g_ref[0].astype(f32))).astype(bf16)


def _dilated_attention(z, *, batch, seq, group0):
    CH = DIL_CHUNK
    assert seq % CH == 0
    nc = seq // CH
    P = PAIRS_DIL
    T = batch * seq
    g0 = group0

    def cur(off):
        return lambda b, p, c: (g0 + off * P + p, b * nc + c, 0)

    def prev(off):
        return lambda b, p, c: (g0 + off * P + p, b * nc + jnp.maximum(c - 1, 0), 0)

    blk = (1, CH, LANE)
    return pl.pallas_call(
        _dil_kernel,
        grid=(batch, P, nc),
        in_specs=[
            pl.BlockSpec(blk, cur(0)),
            pl.BlockSpec(blk, prev(1)), pl.BlockSpec(blk, cur(1)),
            pl.BlockSpec(blk, prev(2)), pl.BlockSpec(blk, cur(2)),
            pl.BlockSpec(blk, cur(3)),
        ],
        out_specs=pl.BlockSpec(blk, lambda b, p, c: (p, b * nc + c, 0)),
        out_shape=jax.ShapeDtypeStruct((P, T, LANE), bf16),
        scratch_shapes=[
            pltpu.VMEM((CH, LANE), f32),
            pltpu.VMEM((2 * CH, LANE), f32),
            pltpu.VMEM((2 * CH, LANE), f32),
            pltpu.VMEM((CH, LANE), f32),
            pltpu.VMEM((2 * CH, LANE), f32),
            pltpu.VMEM((2 * CH, LANE), f32),
            pltpu.VMEM((len(DIL_PATTERNS), CH, LANE), f32),
            pltpu.VMEM((len(DIL_PATTERNS), CH, LANE), f32),
        ],
        compiler_params=pltpu.CompilerParams(
            dimension_semantics=("arbitrary", "arbitrary", "arbitrary")),
        name="dilated_attention",
    )(z, z, z, z, z, z)


def _ret_kernel(q_ref, k_ref, v_ref, g_ref, intra_ref, qd_ref, kd_ref, cd_ref,
                o_ref, state_ref, *, C, n_chunks):
    @pl.when(pl.program_id(2) == 0)
    def _():
        state_ref[...] = jnp.zeros_like(state_ref)

    nq = RET_QK_DIM // LANE
    nv = RET_V_DIM // LANE
    intra = intra_ref[0]
    qd = qd_ref[0]
    kd = kd_ref[0]
    cd = cd_ref[0]

    st = state_ref[...]
    for ci in range(n_chunks):
        r0 = ci * C
        q = jnp.concatenate([q_ref[c, pl.ds(r0, C), :] for c in range(nq)], axis=1)
        k = jnp.concatenate([k_ref[c, pl.ds(r0, C), :] for c in range(nq)], axis=1)
        v = jnp.concatenate([v_ref[c, pl.ds(r0, C), :] for c in range(nv)], axis=1)
        s = lax.dot_general(q, k, (((1,), (1,)), ((), ())),
                            preferred_element_type=f32) * intra
        inner = jnp.dot(s.astype(bf16), v, preferred_element_type=f32)
        cross = jnp.dot(q, st.astype(bf16), preferred_element_type=f32) * qd
        kdec = (k.astype(f32) * kd).astype(bf16)
        upd = lax.dot_general(kdec, v, (((0,), (0,)), ((), ())),
                              preferred_element_type=f32)
        st = st * cd + upd
        if ci == n_chunks - 1:
            state_ref[...] = st
        y = inner + cross
        mu = jnp.mean(y, axis=-1, keepdims=True)
        yc = y - mu
        var = jnp.mean(yc * yc, axis=-1, keepdims=True)
        yn = yc * lax.rsqrt(var + EPS)
        g = jnp.concatenate([g_ref[c, pl.ds(r0, C), :] for c in range(nv)],
                            axis=1).astype(f32)
        out = (yn * _silu(g)).astype(bf16)
        for c in range(nv):
            o_ref[c, pl.ds(r0, C), :] = out[:, c * LANE:(c + 1) * LANE]


def _retention(z, *, batch, seq, C=RET_CHUNK, tr=1024):
    tr = min(tr, seq)
    nr = seq // tr
    H = N_HEADS_RET
    T = batch * seq
    nq = RET_QK_DIM // LANE
    nv = RET_V_DIM // LANE
    kq0 = H
    v0 = 2 * H * nq // nv
    g0 = v0 + H

    log_gamma = jnp.log1p(-jnp.power(2.0, -5.0 - jnp.arange(H, dtype=f32)))
    pos = jnp.arange(C, dtype=f32)
    rel = pos[:, None] - pos[None, :]
    intra = jnp.where(rel >= 0,
                      jnp.exp(log_gamma[:, None, None] * jnp.maximum(rel, 0.0)), 0.0)
    q_decay = jnp.exp(log_gamma[:, None] * (pos + 1.0))[:, :, None]
    k_decay = jnp.exp(log_gamma[:, None] * (C - 1.0 - pos))[:, :, None]
    chunk_decay = jnp.exp(log_gamma * C)[:, None, None]

    return pl.pallas_call(
        functools.partial(_ret_kernel, C=C, n_chunks=tr // C),
        grid=(batch, H, nr),
        in_specs=[
            pl.BlockSpec((nq, tr, LANE), lambda b, h, i: (h, b * nr + i, 0)),
            pl.BlockSpec((nq, tr, LANE), lambda b, h, i: (kq0 + h, b * nr + i, 0)),
            pl.BlockSpec((nv, tr, LANE), lambda b, h, i: (v0 + h, b * nr + i, 0)),
            pl.BlockSpec((nv, tr, LANE), lambda b, h, i: (g0 + h, b * nr + i, 0)),
            pl.BlockSpec((1, C, C), lambda b, h, i: (h, 0, 0)),
            pl.BlockSpec((1, C, 1), lambda b, h, i: (h, 0, 0)),
            pl.BlockSpec((1, C, 1), lambda b, h, i: (h, 0, 0)),
            pl.BlockSpec((1, 1, 1), lambda b, h, i: (h, 0, 0)),
        ],
        out_specs=pl.BlockSpec((nv, tr, LANE), lambda b, h, i: (h, b * nr + i, 0)),
        out_shape=jax.ShapeDtypeStruct((H * nv, T, LANE), bf16),
        scratch_shapes=[pltpu.VMEM((RET_QK_DIM, RET_V_DIM), f32)],
        compiler_params=pltpu.CompilerParams(
            dimension_semantics=("arbitrary", "arbitrary", "arbitrary")),
        name="retention",
    )(z, z, z, z, intra, q_decay, k_decay, chunk_decay)


def _out_proj_kernel(*refs, n_y, final_norm):
    y_refs = refs[:n_y]
    w_ref, x_ref = refs[n_y], refs[n_y + 1]
    g_ref = refs[n_y + 2] if final_norm else None
    o_ref = refs[-1]
    y = jnp.concatenate([r[c] for r in y_refs for c in range(r.shape[0])], axis=1)
    out = x_ref[...] + jnp.dot(y, w_ref[...], preferred_element_type=f32)
    if final_norm:
        ms = jnp.mean(out * out, axis=-1, keepdims=True)
        out = out * lax.rsqrt(ms + EPS) * g_ref[...]
    o_ref[...] = out


def _out_proj(ys, w, x2d, *, gain=None, tm=512, name):
    T, D = x2d.shape
    tm = min(tm, T)
    in_specs = [pl.BlockSpec((y.shape[0], tm, LANE), lambda i: (0, i, 0)) for y in ys]
    in_specs += [pl.BlockSpec(w.shape, lambda i: (0, 0)),
                 pl.BlockSpec((tm, D), lambda i: (i, 0))]
    args = list(ys) + [w, x2d]
    if gain is not None:
        in_specs.append(pl.BlockSpec((1, D), lambda i: (0, 0)))
        args.append(gain.reshape(1, D).astype(f32))
    return pl.pallas_call(
        functools.partial(_out_proj_kernel, n_y=len(ys), final_norm=gain is not None),
        grid=(T // tm,),
        in_specs=in_specs,
        out_specs=pl.BlockSpec((tm, D), lambda i: (i, 0)),
        out_shape=jax.ShapeDtypeStruct((T, D), f32),
        compiler_params=pltpu.CompilerParams(dimension_semantics=("arbitrary",)),
        name=name,
    )(*args)


def _even_layer(x2d, g_norm, w_in, b_f, w_out, *, batch, seq):
    D = x2d.shape[1]
    a_w = 4 * WIDTH_FOX
    nf = N_HEADS_FOX
    scale = HEAD_DIM ** -0.5
    col_scale = jnp.ones((a_w + 4 * WIDTH_DIL,), f32)
    col_scale = col_scale.at[:WIDTH_FOX].set(scale * LOG2E)
    col_scale = col_scale.at[a_w:a_w + WIDTH_DIL].set(scale * LOG2E)
    w_main = jnp.concatenate([w_in[:, :a_w], w_in[:, a_w + nf:]], axis=1) * col_scale
    w_f = jnp.pad(w_in[:, a_w:a_w + nf], ((0, 0), (0, LANE - nf)))
    z, f2d = _in_proj(x2d, g_norm, w_main.astype(bf16), seq=seq, wf=w_f.astype(bf16),
                      name="even_in_proj")
    b_pad = jnp.pad(b_f.astype(f32), (0, LANE - nf)).reshape(1, LANE)
    qb, kb, c2 = _forget_cumsum(f2d, b_pad, batch=batch, seq=seq)
    ya = _fox_attention(z, qb, kb, c2, batch=batch, seq=seq)
    yb = _dilated_attention(z, batch=batch, seq=seq, group0=4 * PAIRS_FOX)
    return _out_proj([ya, yb], w_out.astype(bf16), x2d, name="even_out_proj")


def _odd_layer(x2d, g_norm, w_in, w_out, *, batch, seq, final_gain):
    H, dk = N_HEADS_RET, RET_QK_DIM
    qk_w = H * dk
    half = dk // 2
    perm = (jnp.arange(H)[:, None, None] * dk
            + jnp.arange(2)[None, :, None]
            + 2 * jnp.arange(half)[None, None, :]).reshape(-1)
    w_q = w_in[:, :qk_w][:, perm]
    w_k = w_in[:, qk_w:2 * qk_w][:, perm] * (dk ** -0.5)
    w_all = jnp.concatenate([w_q, w_k, w_in[:, 2 * qk_w:]], axis=1).astype(bf16)

    inv = 1.0 / (ROT_BASE ** jnp.linspace(0.0, 1.0, half, dtype=f32))
    ang = jnp.arange(seq, dtype=f32)[:, None] * inv[None, :]
    rot = (jnp.cos(ang), jnp.sin(ang))
    (z,) = _in_proj(x2d, g_norm, w_all, seq=seq, rot=rot, n_rot_blocks=2 * qk_w // 1024,
                    name="odd_in_proj")
    yc = _retention(z, batch=batch, seq=seq)
    return _out_proj([yc], w_out.astype(bf16), x2d, gain=final_gain, name="odd_out_proj")


def kernel(x, even_norm, even_w_in, even_b_f, even_w_out, odd_norm, odd_w_in, odd_w_out, final_norm):
    batch, seq, D = x.shape
    x2d = x.reshape(batch * seq, D)
    x2d = _even_layer(x2d, even_norm[0], even_w_in[0], even_b_f[0], even_w_out[0],
                      batch=batch, seq=seq)
    out = _odd_layer(x2d, odd_norm[0], odd_w_in[0], odd_w_out[0],
                     batch=batch, seq=seq, final_gain=final_norm)
    return out.reshape(batch, seq, D)
```

```python
import functools

import numpy as np
import jax
import jax.numpy as jnp
from jax import lax
from jax.experimental import pallas as pl
from jax.experimental.pallas import tpu as pltpu

f32 = jnp.float32
bf16 = jnp.bfloat16

LANE = 128
HEAD_DIM = 64
N_HEADS_FOX = 16
N_HEADS_DIL = 16
WIDTH_FOX = N_HEADS_FOX * HEAD_DIM
WIDTH_DIL = N_HEADS_DIL * HEAD_DIM
DIL_BLOCK = 128
DIL_PATTERNS = ((128, 1), (512, 4), (2048, 16))
DIL_CHUNK = 2048
DIL_DEINTERLEAVE = 4
DIL_UNROLL = 8
N_HEADS_RET = 4
RET_QK_DIM = 256
RET_V_DIM = 512
RET_CHUNK = 256
ROT_BASE = 10000.0
EPS = 1e-6
NEG = -1e30

LOG2E = 1.4426950408889634
N_PIECES = 3
BIAS_LANES = 8
FOX_ROW_BLOCK = 32
FOX_SKIP_LOG2 = 160.0

IN_PROJ_BN = 2048

PAIRS_FOX = WIDTH_FOX // LANE
PAIRS_DIL = WIDTH_DIL // LANE


def _silu(g):
    return g / (1.0 + jnp.exp(-g))


def _in_proj_kernel(*refs, n_groups, with_forget, n_rot_blocks):
    it = iter(refs)
    x_ref, g_ref, w_ref = next(it), next(it), next(it)
    wf_ref = next(it) if with_forget else None
    cos_ref = sin_ref = None
    if n_rot_blocks:
        cos_ref, sin_ref = next(it), next(it)
    o_ref = next(it)
    f_ref = next(it) if with_forget else None
    h_ref = next(it)

    j = pl.program_id(1)

    @pl.when(j == 0)
    def _():
        x = x_ref[...]
        ms = jnp.mean(x * x, axis=-1, keepdims=True)
        h = (x * lax.rsqrt(ms + EPS) * g_ref[...]).astype(bf16)
        h_ref[...] = h
        if with_forget:
            f_ref[...] = jnp.dot(h, wf_ref[...], preferred_element_type=f32)

    acc = jnp.dot(h_ref[...], w_ref[...], preferred_element_type=f32)

    def store_plain():
        for c in range(n_groups):
            o_ref[c] = acc[:, c * LANE:(c + 1) * LANE].astype(bf16)

    if not n_rot_blocks:
        store_plain()
        return

    @pl.when(j < n_rot_blocks)
    def _():
        cos, sin = cos_ref[...], sin_ref[...]
        for c in range(0, n_groups, 2):
            x1 = acc[:, c * LANE:(c + 1) * LANE]
            x2 = acc[:, (c + 1) * LANE:(c + 2) * LANE]
            o_ref[c] = (x1 * cos - x2 * sin).astype(bf16)
            o_ref[c + 1] = (x1 * sin + x2 * cos).astype(bf16)

    @pl.when(j >= n_rot_blocks)
    def _():
        store_plain()


def _in_proj(x2d, gain, w, *, seq, wf=None, rot=None, n_rot_blocks=0, tm=1024, bn=IN_PROJ_BN,
             name):
    T, D = x2d.shape
    N = w.shape[1]
    tm = min(tm, seq)
    n_groups = bn // LANE
    grid = (T // tm, N // bn)
    in_specs = [
        pl.BlockSpec((tm, D), lambda i, j: (i, 0)),
        pl.BlockSpec((1, D), lambda i, j: (0, 0)),
        pl.BlockSpec((D, bn), lambda i, j: (0, j)),
    ]
    args = [x2d, gain.reshape(1, D).astype(f32), w]
    out_shape = [jax.ShapeDtypeStruct((N // LANE, T, LANE), bf16)]
    out_specs = [pl.BlockSpec((n_groups, tm, LANE), lambda i, j: (j, i, 0))]
    if wf is not None:
        in_specs.append(pl.BlockSpec((D, LANE), lambda i, j: (0, 0)))
        args.append(wf)
        out_shape.append(jax.ShapeDtypeStruct((T, LANE), f32))
        out_specs.append(pl.BlockSpec((tm, LANE), lambda i, j: (i, 0)))
    if n_rot_blocks:
        spb = seq // tm
        for t in rot:
            in_specs.append(pl.BlockSpec((tm, LANE), lambda i, j: (i % spb, 0)))
            args.append(t)
    kern = functools.partial(_in_proj_kernel, n_groups=n_groups,
                             with_forget=wf is not None, n_rot_blocks=n_rot_blocks)
    outs = pl.pallas_call(
        kern,
        grid=grid,
        in_specs=in_specs,
        out_specs=out_specs,
        out_shape=out_shape,
        scratch_shapes=[pltpu.VMEM((tm, D), bf16)],
        compiler_params=pltpu.CompilerParams(
            dimension_semantics=("arbitrary", "arbitrary")),
        name=name,
    )(*args)
    return outs


def _split_bf16(x):
    p0 = x.astype(bf16)
    r1 = x - p0.astype(f32)
    p1 = r1.astype(bf16)
    p2 = (r1 - p1.astype(f32)).astype(bf16)
    return jnp.concatenate([p0, p1, p2], axis=1)


def _cumsum_kernel(f_ref, b_ref, pq_ref, oq_ref, pk_ref, ok_ref, qb_ref, kb_ref, c2_ref,
                   carry_ref, *, tc):
    @pl.when(pl.program_id(1) == 0)
    def _():
        carry_ref[...] = jnp.zeros_like(carry_ref)

    z = f_ref[...] + b_ref[...]
    lf = jnp.minimum(z, 0.0) - jnp.log1p(jnp.exp(-jnp.abs(z)))
    row = lax.broadcasted_iota(jnp.int32, (tc, tc), 0)
    col = lax.broadcasted_iota(jnp.int32, (tc, tc), 1)
    tri = jnp.where(col <= row, 1.0, 0.0).astype(bf16)
    cs = jnp.dot(tri, _split_bf16(lf), preferred_element_type=f32)
    c = (cs[:, :LANE] + cs[:, LANE:2 * LANE] + cs[:, 2 * LANE:]) + carry_ref[...]
    carry_ref[...] = c[tc - 1:tc, :]

    c2 = c * LOG2E
    c2_ref[...] = c2
    pieces = _split_bf16(c2)
    qb = jnp.dot(pieces, pq_ref[...], preferred_element_type=f32) + oq_ref[...]
    kb = jnp.dot(pieces, pk_ref[...], preferred_element_type=f32) + ok_ref[...]
    qb_ref[...] = qb.astype(bf16)
    kb_ref[...] = kb.astype(bf16)


def _bias_placement():
    assert N_HEADS_FOX * BIAS_LANES == LANE and 2 * N_PIECES <= BIAS_LANES
    pq = np.zeros((N_PIECES * LANE, LANE), np.float32)
    pk = np.zeros((N_PIECES * LANE, LANE), np.float32)
    oq = np.zeros((1, LANE), np.float32)
    ok = np.zeros((1, LANE), np.float32)
    for head in range(N_HEADS_FOX):
        base = BIAS_LANES * head
        for piece in range(N_PIECES):
            pq[piece * LANE + head, base + piece] = 1.0
            ok[0, base + piece] = 1.0
            oq[0, base + N_PIECES + piece] = 1.0
            pk[piece * LANE + head, base + N_PIECES + piece] = -1.0
    return (jnp.asarray(pq, bf16), jnp.asarray(oq, f32),
            jnp.asarray(pk, bf16), jnp.asarray(ok, f32))


def _forget_cumsum(f2d, b_pad, *, batch, seq, tc=512):
    tc = min(tc, seq)
    nb = seq // tc
    T = batch * seq
    pq, oq, pk, ok = _bias_placement()
    full = lambda a: pl.BlockSpec(a.shape, lambda b, i: (0, 0))
    row_blk = pl.BlockSpec((tc, LANE), lambda b, i: (b * nb + i, 0))
    return pl.pallas_call(
        functools.partial(_cumsum_kernel, tc=tc),
        grid=(batch, nb),
        in_specs=[row_blk, full(b_pad), full(pq), full(oq), full(pk), full(ok)],
        out_specs=[row_blk, row_blk, row_blk],
        out_shape=[jax.ShapeDtypeStruct((T, LANE), bf16)] * 2
        + [jax.ShapeDtypeStruct((T, LANE), f32)],
        scratch_shapes=[pltpu.VMEM((1, LANE), f32)],
        compiler_params=pltpu.CompilerParams(
            dimension_semantics=("arbitrary", "arbitrary")),
        name="forget_cumsum",
    )(f2d, b_pad, pq, oq, pk, ok)


def _fox_kernel(q_ref, k_ref, v_ref, g_ref, qb_ref, kb_ref, cs_ref, ce_ref, o_ref,
                s_ref, p_ref, alpha_ref, m_ref, acc_ref, j0_ref, *, tq, n_blocks):
    qi = pl.program_id(2)
    lane = lax.broadcasted_iota(jnp.int32, (1, LANE), 1)
    lo = lane < HEAD_DIM
    own = (lo, jnp.logical_not(lo))
    head0 = 2 * pl.program_id(1)
    own_bias = [(lane >= (head0 + h) * BIAS_LANES) & (lane < (head0 + h + 1) * BIAS_LANES)
                for h in range(2)]
    q, qb = q_ref[0, pl.ds(pl.multiple_of(qi * tq, tq), tq), :], qb_ref[...]
    zero = jnp.zeros_like(q)
    q2 = [jnp.concatenate([jnp.where(own[h], q, zero), jnp.where(own_bias[h], qb, zero)],
                          axis=1) for h in range(2)]
    rb = FOX_ROW_BLOCK
    row_in_blk = lax.broadcasted_iota(jnp.int32, (rb, LANE), 0)
    col = lax.broadcasted_iota(jnp.int32, (rb, LANE), 1)

    sel_row = lax.broadcasted_iota(jnp.int32, (2 * LANE, LANE), 0) % LANE
    sel_col = lax.broadcasted_iota(jnp.int32, (2 * LANE, LANE), 1)
    sel = jnp.where(sel_col == (sel_row >= HEAD_DIM).astype(jnp.int32), 1.0, 0.0).astype(bf16)

    def max_sq_norm(x):
        xf = x.astype(f32)
        sq = xf * xf
        hi = sq.astype(bf16)
        lo = (sq - hi.astype(f32)).astype(bf16)
        n2 = jnp.dot(jnp.concatenate([hi, lo], axis=1), sel, preferred_element_type=f32)
        return jnp.max(n2, axis=0, keepdims=True)

    @pl.when(qi == 0)
    def _():
        blocks = [slice(j * tq, (j + 1) * tq) for j in range(n_blocks)]
        k_max = functools.reduce(jnp.maximum, [max_sq_norm(k_ref[0, b, :]) for b in blocks])
        q_max = jnp.concatenate([max_sq_norm(q_ref[0, b, :]) for b in blocks], axis=0)
        qk = 2.0 * jnp.sqrt(q_max * k_max)
        eye = (lax.broadcasted_iota(jnp.int32, (n_blocks, n_blocks), 0)
               == lax.broadcasted_iota(jnp.int32, (n_blocks, n_blocks), 1))
        earlier = (lax.broadcasted_iota(jnp.int32, (n_blocks, n_blocks), 1)
                   < lax.broadcasted_iota(jnp.int32, (n_blocks, n_blocks), 0))
        for h in range(2):
            on_head = lane == head0 + h
            col_of = lambda x, m: jnp.sum(jnp.where(m, x, 0.0), axis=1, keepdims=True)
            hi_i = col_of(qk, lane == h) + col_of(cs_ref[0], on_head)
            ce_j = jnp.sum(jnp.where(eye, col_of(ce_ref[0], on_head), 0.0),
                           axis=0, keepdims=True)
            skip = earlier & (hi_i - ce_j < -FOX_SKIP_LOG2)
            n_skip = jnp.sum(jnp.where(skip, 1.0, 0.0), axis=1, keepdims=True)
            for i in range(n_blocks):
                j0_ref[h, i] = n_skip[i, 0].astype(jnp.int32)

    m_ref[...] = jnp.full(m_ref.shape, NEG, f32)
    acc_ref[...] = jnp.zeros(acc_ref.shape, f32)

    def scores(j, slot, heads):
        r0 = pl.multiple_of(j * tq, tq)
        k2 = jnp.concatenate([k_ref[0, pl.ds(r0, tq), :], kb_ref[pl.ds(r0, tq), :]],
                             axis=1)
        for h in heads:
            s_ref[slot, h] = lax.dot_general(q2[h], k2, (((1,), (1,)), ((), ())),
                                             preferred_element_type=f32)

    def softmax(slot, masked, heads):
        for h in heads:
            for r in range(tq // rb):
                rows = pl.ds(r * rb, rb)
                tiles = []
                for t in range(tq // LANE):
                    s = s_ref[slot, h, rows, t * LANE:(t + 1) * LANE]
                    if masked:
                        s = jnp.where(col + t * LANE <= row_in_blk + r * rb, s, NEG)
                    tiles.append(s)
                mx = functools.reduce(jnp.maximum, tiles)
                m_old = m_ref[h, rows, :]
                m_new = jnp.maximum(m_old, jnp.max(mx, axis=-1, keepdims=True))
                m_ref[h, rows, :] = m_new
                alpha_ref[slot, h, rows, :] = jnp.exp2(m_old - m_new)
                for t, s in enumerate(tiles):
                    p_ref[slot, h, rows, t * LANE:(t + 1) * LANE] = (
                        jnp.exp2(s - m_new).astype(bf16))

    def accumulate(j, slot, heads):
        r0 = pl.multiple_of(j * tq, tq)
        v = v_ref[0, pl.ds(r0, tq), :]
        for h in heads:
            vh = jnp.where(own[h], v, jnp.ones_like(v))
            acc_ref[h] = alpha_ref[slot, h] * acc_ref[h] + jnp.dot(
                p_ref[slot, h], vh, preferred_element_type=f32)

    def step_single(j, masked, heads):
        scores(j, 0, heads)
        softmax(0, masked, heads)
        accumulate(j, 0, heads)
        return 0

    def run_blocks(j_lo, n, heads):
        def step_pair(jj, _):
            j = j_lo + 2 * jj
            scores(j, 0, heads)
            scores(j + 1, 1, heads)
            softmax(0, False, heads)
            accumulate(j, 0, heads)
            softmax(1, False, heads)
            accumulate(j + 1, 1, heads)
            return 0

        lax.fori_loop(0, n // 2, step_pair, 0)
        lax.fori_loop(0, n % 2, lambda _, c: step_single(j_lo + n - 1, False, heads), 0)

    j0 = [j0_ref[h, qi] for h in range(2)]
    j_both = jnp.maximum(j0[0], j0[1])
    run_blocks(j0[0], j_both - j0[0], (0,))
    run_blocks(j0[1], j_both - j0[1], (1,))
    run_blocks(j_both, qi - j_both, (0, 1))
    step_single(qi, True, (0, 1))

    acc0, acc1 = acc_ref[0], acc_ref[1]
    num = jnp.where(lo, acc0, acc1)
    den = pltpu.roll(jnp.where(lo, acc1, acc0), HEAD_DIM, axis=1)
    o_ref[0] = (num / den * _silu(g_ref[0].astype(f32))).astype(bf16)


def _fox_attention(z, qb, kb, c2, *, batch, seq, tq=512):
    tq = min(tq, seq)
    nq = seq // tq
    P = PAIRS_FOX
    T = batch * seq
    c_blocks = c2.reshape(batch, nq, tq, LANE)
    c_first, c_last = c_blocks[:, :, 0, :], c_blocks[:, :, tq - 1, :]
    q_blk = lambda g0: pl.BlockSpec((1, tq, LANE), lambda b, p, i: (g0 + p, b * nq + i, 0))
    kv_blk = lambda g0: pl.BlockSpec((1, seq, LANE), lambda b, p, i: (g0 + p, b, 0))
    c_blk = pl.BlockSpec((1, nq, LANE), lambda b, p, i: (b, 0, 0))
    return pl.pallas_call(
        functools.partial(_fox_kernel, tq=tq, n_blocks=nq),
        grid=(batch, P, nq),
        in_specs=[kv_blk(0), kv_blk(P), kv_blk(2 * P), q_blk(3 * P),
                  pl.BlockSpec((tq, LANE), lambda b, p, i: (b * nq + i, 0)),
                  pl.BlockSpec((seq, LANE), lambda b, p, i: (b, 0)),
                  c_blk, c_blk],
        out_specs=q_blk(0),
        out_shape=jax.ShapeDtypeStruct((P, T, LANE), bf16),
        scratch_shapes=[
            pltpu.VMEM((2, 2, tq, tq), f32),
            pltpu.VMEM((2, 2, tq, tq), bf16),
            pltpu.VMEM((2, 2, tq, LANE), f32),
            pltpu.VMEM((2, tq, LANE), f32),
            pltpu.VMEM((2, tq, LANE), f32),
            pltpu.SMEM((2, nq), jnp.int32),
        ],
        compiler_params=pltpu.CompilerParams(
            dimension_semantics=("arbitrary", "arbitrary", "arbitrary")),
        name="fox_attention",
    )(z, z, z, z, qb, kb, c_first, c_last)


def _dil_kernel(q_ref, kp_ref, kc_ref, vp_ref, vc_ref, g_ref, o_ref,
                qs_ref, ks_ref, vs_ref, q4_ref, k4_ref, v4_ref, os_ref, ls_ref):
    chunk = pl.program_id(2)
    CH, BLK, DEI = DIL_CHUNK, DIL_BLOCK, DIL_DEINTERLEAVE
    qs_ref[...] = q_ref[0].astype(f32)
    ks_ref[0:CH, :] = kp_ref[0].astype(f32)
    ks_ref[CH:2 * CH, :] = kc_ref[0].astype(f32)
    vs_ref[0:CH, :] = vp_ref[0].astype(f32)
    vs_ref[CH:2 * CH, :] = vc_ref[0].astype(f32)
    for res in range(DEI):
        nq, nk = CH // DEI, 2 * CH // DEI
        q4_ref[res * nq:(res + 1) * nq, :] = qs_ref[pl.ds(res, nq, stride=DEI), :]
        k4_ref[res * nk:(res + 1) * nk, :] = ks_ref[pl.ds(res, nk, stride=DEI), :]
        v4_ref[res * nk:(res + 1) * nk, :] = vs_ref[pl.ds(res, nk, stride=DEI), :]

    lane = lax.broadcasted_iota(jnp.int32, (1, LANE), 1)
    lo = lane < HEAD_DIM
    hi = jnp.logical_not(lo)
    qi = lax.broadcasted_iota(jnp.int32, (BLK, 2 * BLK), 0)
    kj = lax.broadcasted_iota(jnp.int32, (BLK, 2 * BLK), 1)
    dist = BLK + qi - kj
    n_tiles = CH // BLK

    for p, (window, r) in enumerate(DIL_PATTERNS):
        n_keys = window // r
        band = (dist >= 0) & (dist <= n_keys)
        bias_band = jnp.where(band, 0.0, NEG)
        bias_first = jnp.where(band & (kj >= BLK), 0.0, NEG)
        span = r * BLK

        def tile(t, r=r, span=span, p=p, bias_band=bias_band, bias_first=bias_first):
            sidx = t // r
            qstart = sidx * span + t % r
            kstart = CH + qstart - span
            no_prev = jnp.logical_and(chunk == 0, sidx == 0)
            bias = jnp.where(no_prev, bias_first, bias_band)
            if r % DEI == 0:
                rr = r // DEI
                res4, sub = (t % r) % DEI, (t % r) // DEI
                q0 = res4 * (CH // DEI) + sidx * (span // DEI) + sub
                k0 = res4 * (2 * CH // DEI) + CH // DEI + sidx * (span // DEI) + sub - span // DEI
                stride = rr if rr > 1 else None
                qt = q4_ref[pl.ds(q0, BLK, stride=stride), :].astype(bf16)
                kt = k4_ref[pl.ds(k0, 2 * BLK, stride=stride), :].astype(bf16)
                vt = v4_ref[pl.ds(k0, 2 * BLK, stride=stride), :].astype(bf16)
            else:
                qt = qs_ref[pl.ds(qstart, BLK, stride=r), :].astype(bf16)
                kt = ks_ref[pl.ds(kstart, 2 * BLK, stride=r), :].astype(bf16)
                vt = vs_ref[pl.ds(kstart, 2 * BLK, stride=r), :].astype(bf16)
            res, ms = [], []
            for own in (lo, hi):
                qh = jnp.where(own, qt, jnp.zeros_like(qt))
                vh = jnp.where(own, vt, jnp.ones_like(vt))
                s = lax.dot_general(qh, kt, (((1,), (1,)), ((), ())),
                                    preferred_element_type=f32) + bias
                m = jnp.max(s, axis=-1, keepdims=True)
                e = jnp.exp2(s - m).astype(bf16)
                res.append(jnp.dot(e, vh, preferred_element_type=f32))
                ms.append(m)
            num = jnp.where(lo, res[0], res[1])
            den = pltpu.roll(jnp.where(lo, res[1], res[0]), HEAD_DIM, axis=1)
            os_ref[p, pl.ds(qstart, BLK, stride=r), :] = num / den
            ls_ref[p, pl.ds(qstart, BLK, stride=r), :] = (
                jnp.where(lo, ms[0], ms[1]) + jnp.log2(den))

        def group(g, _, tile=tile):
            for u in range(DIL_UNROLL):
                tile(g * DIL_UNROLL + u)
            return 0

        lax.fori_loop(0, n_tiles // DIL_UNROLL, group, 0)

    l0, l1, l2 = ls_ref[0], ls_ref[1], ls_ref[2]
    mx = jnp.maximum(jnp.maximum(l0, l1), l2)
    w0, w1, w2 = jnp.exp2(l0 - mx), jnp.exp2(l1 - mx), jnp.exp2(l2 - mx)
    y = (w0 * os_ref[0] + w1 * os_ref[1] + w2 * os_ref[2]) / (w0 + w1 + w2)
    o_ref[0] = (y * _silu(g_ref[0].astype(f32))).astype(bf16)


def _dilated_attention(z, *, batch, seq, group0):
    CH = DIL_CHUNK
    assert seq % CH == 0
    nc = seq // CH
    P = PAIRS_DIL
    T = batch * seq
    g0 = group0

    def cur(off):
        return lambda b, p, c: (g0 + off * P + p, b * nc + c, 0)

    def prev(off):
        return lambda b, p, c: (g0 + off * P + p, b * nc + jnp.maximum(c - 1, 0), 0)

    blk = (1, CH, LANE)
    return pl.pallas_call(
        _dil_kernel,
        grid=(batch, P, nc),
        in_specs=[
            pl.BlockSpec(blk, cur(0)),
            pl.BlockSpec(blk, prev(1)), pl.BlockSpec(blk, cur(1)),
            pl.BlockSpec(blk, prev(2)), pl.BlockSpec(blk, cur(2)),
            pl.BlockSpec(blk, cur(3)),
        ],
        out_specs=pl.BlockSpec(blk, lambda b, p, c: (p, b * nc + c, 0)),
        out_shape=jax.ShapeDtypeStruct((P, T, LANE), bf16),
        scratch_shapes=[
            pltpu.VMEM((CH, LANE), f32),
            pltpu.VMEM((2 * CH, LANE), f32),
            pltpu.VMEM((2 * CH, LANE), f32),
            pltpu.VMEM((CH, LANE), f32),
            pltpu.VMEM((2 * CH, LANE), f32),
            pltpu.VMEM((2 * CH, LANE), f32),
            pltpu.VMEM((len(DIL_PATTERNS), CH, LANE), f32),
            pltpu.VMEM((len(DIL_PATTERNS), CH, LANE), f32),
        ],
        compiler_params=pltpu.CompilerParams(
            dimension_semantics=("arbitrary", "arbitrary", "arbitrary")),
        name="dilated_attention",
    )(z, z, z, z, z, z)


def _ret_kernel(q_ref, k_ref, v_ref, g_ref, intra_ref, qd_ref, kd_ref, cd_ref,
                o_ref, state_ref, *, C, n_chunks):
    @pl.when(pl.program_id(2) == 0)
    def _():
        state_ref[...] = jnp.zeros_like(state_ref)

    nq = RET_QK_DIM // LANE
    nv = RET_V_DIM // LANE
    intra = intra_ref[0]
    qd = qd_ref[0]
    kd = kd_ref[0]
    cd = cd_ref[0]

    st = state_ref[...]
    for ci in range(n_chunks):
        r0 = ci * C
        q = jnp.concatenate([q_ref[c, pl.ds(r0, C), :] for c in range(nq)], axis=1)
        k = jnp.concatenate([k_ref[c, pl.ds(r0, C), :] for c in range(nq)], axis=1)
        v = jnp.concatenate([v_ref[c, pl.ds(r0, C), :] for c in range(nv)], axis=1)
        s = lax.dot_general(q, k, (((1,), (1,)), ((), ())),
                            preferred_element_type=f32) * intra
        inner = jnp.dot(s.astype(bf16), v, preferred_element_type=f32)
        cross = jnp.dot(q, st.astype(bf16), preferred_element_type=f32) * qd
        kdec = (k.astype(f32) * kd).astype(bf16)
        upd = lax.dot_general(kdec, v, (((0,), (0,)), ((), ())),
                              preferred_element_type=f32)
        st = st * cd + upd
        if ci == n_chunks - 1:
            state_ref[...] = st
        y = inner + cross
        mu = jnp.mean(y, axis=-1, keepdims=True)
        yc = y - mu
        var = jnp.mean(yc * yc, axis=-1, keepdims=True)
        yn = yc * lax.rsqrt(var + EPS)
        g = jnp.concatenate([g_ref[c, pl.ds(r0, C), :] for c in range(nv)],
                            axis=1).astype(f32)
        out = (yn * _silu(g)).astype(bf16)
        for c in range(nv):
            o_ref[c, pl.ds(r0, C), :] = out[:, c * LANE:(c + 1) * LANE]


def _retention(z, *, batch, seq, C=RET_CHUNK, tr=1024):
    tr = min(tr, seq)
    nr = seq // tr
    H = N_HEADS_RET
    T = batch * seq
    nq = RET_QK_DIM // LANE
    nv = RET_V_DIM // LANE
    kq0 = H
    v0 = 2 * H * nq // nv
    g0 = v0 + H

    log_gamma = jnp.log1p(-jnp.power(2.0, -5.0 - jnp.arange(H, dtype=f32)))
    pos = jnp.arange(C, dtype=f32)
    rel = pos[:, None] - pos[None, :]
    intra = jnp.where(rel >= 0,
                      jnp.exp(log_gamma[:, None, None] * jnp.maximum(rel, 0.0)), 0.0)
    q_decay = jnp.exp(log_gamma[:, None] * (pos + 1.0))[:, :, None]
    k_decay = jnp.exp(log_gamma[:, None] * (C - 1.0 - pos))[:, :, None]
    chunk_decay = jnp.exp(log_gamma * C)[:, None, None]

    return pl.pallas_call(
        functools.partial(_ret_kernel, C=C, n_chunks=tr // C),
        grid=(batch, H, nr),
        in_specs=[
            pl.BlockSpec((nq, tr, LANE), lambda b, h, i: (h, b * nr + i, 0)),
            pl.BlockSpec((nq, tr, LANE), lambda b, h, i: (kq0 + h, b * nr + i, 0)),
            pl.BlockSpec((nv, tr, LANE), lambda b, h, i: (v0 + h, b * nr + i, 0)),
            pl.BlockSpec((nv, tr, LANE), lambda b, h, i: (g0 + h, b * nr + i, 0)),
            pl.BlockSpec((1, C, C), lambda b, h, i: (h, 0, 0)),
            pl.BlockSpec((1, C, 1), lambda b, h, i: (h, 0, 0)),
            pl.BlockSpec((1, C, 1), lambda b, h, i: (h, 0, 0)),
            pl.BlockSpec((1, 1, 1), lambda b, h, i: (h, 0, 0)),
        ],
        out_specs=pl.BlockSpec((nv, tr, LANE), lambda b, h, i: (h, b * nr + i, 0)),
        out_shape=jax.ShapeDtypeStruct((H * nv, T, LANE), bf16),
        scratch_shapes=[pltpu.VMEM((RET_QK_DIM, RET_V_DIM), f32)],
        compiler_params=pltpu.CompilerParams(
            dimension_semantics=("arbitrary", "arbitrary", "arbitrary")),
        name="retention",
    )(z, z, z, z, intra, q_decay, k_decay, chunk_decay)


def _out_proj_kernel(*refs, n_y, final_norm):
    y_refs = refs[:n_y]
    w_ref, x_ref = refs[n_y], refs[n_y + 1]
    g_ref = refs[n_y + 2] if final_norm else None
    o_ref = refs[-1]
    y = jnp.concatenate([r[c] for r in y_refs for c in range(r.shape[0])], axis=1)
    out = x_ref[...] + jnp.dot(y, w_ref[...], preferred_element_type=f32)
    if final_norm:
        ms = jnp.mean(out * out, axis=-1, keepdims=True)
        out = out * lax.rsqrt(ms + EPS) * g_ref[...]
    o_ref[...] = out


def _out_proj(ys, w, x2d, *, gain=None, tm=512, name):
    T, D = x2d.shape
    tm = min(tm, T)
    in_specs = [pl.BlockSpec((y.shape[0], tm, LANE), lambda i: (0, i, 0)) for y in ys]
    in_specs += [pl.BlockSpec(w.shape, lambda i: (0, 0)),
                 pl.BlockSpec((tm, D), lambda i: (i, 0))]
    args = list(ys) + [w, x2d]
    if gain is not None:
        in_specs.append(pl.BlockSpec((1, D), lambda i: (0, 0)))
        args.append(gain.reshape(1, D).astype(f32))
    return pl.pallas_call(
        functools.partial(_out_proj_kernel, n_y=len(ys), final_norm=gain is not None),
        grid=(T // tm,),
        in_specs=in_specs,
        out_specs=pl.BlockSpec((tm, D), lambda i: (i, 0)),
        out_shape=jax.ShapeDtypeStruct((T, D), f32),
        compiler_params=pltpu.CompilerParams(dimension_semantics=("arbitrary",)),
        name=name,
    )(*args)


def _even_layer(x2d, g_norm, w_in, b_f, w_out, *, batch, seq):
    D = x2d.shape[1]
    a_w = 4 * WIDTH_FOX
    nf = N_HEADS_FOX
    scale = HEAD_DIM ** -0.5
    col_scale = jnp.ones((a_w + 4 * WIDTH_DIL,), f32)
    col_scale = col_scale.at[:WIDTH_FOX].set(scale * LOG2E)
    col_scale = col_scale.at[a_w:a_w + WIDTH_DIL].set(scale * LOG2E)
    w_main = jnp.concatenate([w_in[:, :a_w], w_in[:, a_w + nf:]], axis=1) * col_scale
    w_f = jnp.pad(w_in[:, a_w:a_w + nf], ((0, 0), (0, LANE - nf)))
    z, f2d = _in_proj(x2d, g_norm, w_main.astype(bf16), seq=seq, wf=w_f.astype(bf16),
                      name="even_in_proj")
    b_pad = jnp.pad(b_f.astype(f32), (0, LANE - nf)).reshape(1, LANE)
    qb, kb, c2 = _forget_cumsum(f2d, b_pad, batch=batch, seq=seq)
    ya = _fox_attention(z, qb, kb, c2, batch=batch, seq=seq)
    yb = _dilated_attention(z, batch=batch, seq=seq, group0=4 * PAIRS_FOX)
    return _out_proj([ya, yb], w_out.astype(bf16), x2d, name="even_out_proj")


def _odd_layer(x2d, g_norm, w_in, w_out, *, batch, seq, final_gain):
    H, dk = N_HEADS_RET, RET_QK_DIM
    qk_w = H * dk
    half = dk // 2
    perm = (jnp.arange(H)[:, None, None] * dk
            + jnp.arange(2)[None, :, None]
            + 2 * jnp.arange(half)[None, None, :]).reshape(-1)
    w_q = w_in[:, :qk_w][:, perm]
    w_k = w_in[:, qk_w:2 * qk_w][:, perm] * (dk ** -0.5)
    w_all = jnp.concatenate([w_q, w_k, w_in[:, 2 * qk_w:]], axis=1).astype(bf16)

    inv = 1.0 / (ROT_BASE ** jnp.linspace(0.0, 1.0, half, dtype=f32))
    ang = jnp.arange(seq, dtype=f32)[:, None] * inv[None, :]
    rot = (jnp.cos(ang), jnp.sin(ang))
    (z,) = _in_proj(x2d, g_norm, w_all, seq=seq, rot=rot, n_rot_blocks=2 * qk_w // IN_PROJ_BN,
                    name="odd_in_proj")
    yc = _retention(z, batch=batch, seq=seq)
    return _out_proj([yc], w_out.astype(bf16), x2d, gain=final_gain, name="odd_out_proj")


def kernel(x, even_norm, even_w_in, even_b_f, even_w_out, odd_norm, odd_w_in, odd_w_out, final_norm):
    batch, seq, D = x.shape
    x2d = x.reshape(batch * seq, D)
    x2d = _even_layer(x2d, even_norm[0], even_w_in[0], even_b_f[0], even_w_out[0],
                      batch=batch, seq=seq)
    out = _odd_layer(x2d, odd_norm[0], odd_w_in[0], odd_w_out[0],
                     batch=batch, seq=seq, final_gain=final_norm)
    return out.reshape(batch, seq, D)
```

```python
import functools

import numpy as np
import jax
import jax.numpy as jnp
from jax import lax
from jax.experimental import pallas as pl
from jax.experimental.pallas import tpu as pltpu

f32 = jnp.float32
bf16 = jnp.bfloat16

LANE = 128
HEAD_DIM = 64
N_HEADS_FOX = 16
N_HEADS_DIL = 16
WIDTH_FOX = N_HEADS_FOX * HEAD_DIM
WIDTH_DIL = N_HEADS_DIL * HEAD_DIM
DIL_BLOCK = 128
DIL_PATTERNS = ((128, 1), (512, 4), (2048, 16))
DIL_CHUNK = 2048
DIL_DEINTERLEAVE = 4
DIL_UNROLL = 8
N_HEADS_RET = 4
RET_QK_DIM = 256
RET_V_DIM = 512
RET_CHUNK = 256
ROT_BASE = 10000.0
EPS = 1e-6
NEG = -1e30

LOG2E = 1.4426950408889634
N_PIECES = 3
BIAS_LANES = 8
FOX_ROW_BLOCK = 32
FOX_SKIP_LOG2 = 160.0

IN_PROJ_BN = 2048

PAIRS_FOX = WIDTH_FOX // LANE
PAIRS_DIL = WIDTH_DIL // LANE


def _silu(g):
    return g / (1.0 + jnp.exp(-g))


def _in_proj_kernel(*refs, n_groups, with_forget, n_rot_blocks):
    it = iter(refs)
    x_ref, g_ref, w_ref = next(it), next(it), next(it)
    wf_ref = next(it) if with_forget else None
    cos_ref = sin_ref = None
    if n_rot_blocks:
        cos_ref, sin_ref = next(it), next(it)
    o_ref = next(it)
    f_ref = next(it) if with_forget else None
    h_ref = next(it)

    j = pl.program_id(1)

    @pl.when(j == 0)
    def _():
        x = x_ref[...]
        ms = jnp.mean(x * x, axis=-1, keepdims=True)
        h = (x * lax.rsqrt(ms + EPS) * g_ref[...]).astype(bf16)
        h_ref[...] = h
        if with_forget:
            f_ref[...] = jnp.dot(h, wf_ref[...], preferred_element_type=f32)

    acc = jnp.dot(h_ref[...], w_ref[...], preferred_element_type=f32)

    def store_plain():
        for c in range(n_groups):
            o_ref[c] = acc[:, c * LANE:(c + 1) * LANE].astype(bf16)

    if not n_rot_blocks:
        store_plain()
        return

    @pl.when(j < n_rot_blocks)
    def _():
        cos, sin = cos_ref[...], sin_ref[...]
        for c in range(0, n_groups, 2):
            x1 = acc[:, c * LANE:(c + 1) * LANE]
            x2 = acc[:, (c + 1) * LANE:(c + 2) * LANE]
            o_ref[c] = (x1 * cos - x2 * sin).astype(bf16)
            o_ref[c + 1] = (x1 * sin + x2 * cos).astype(bf16)

    @pl.when(j >= n_rot_blocks)
    def _():
        store_plain()


def _in_proj(x2d, gain, w, *, seq, wf=None, rot=None, n_rot_blocks=0, tm=1024, bn=IN_PROJ_BN,
             name):
    T, D = x2d.shape
    N = w.shape[1]
    tm = min(tm, seq)
    n_groups = bn // LANE
    grid = (T // tm, N // bn)
    in_specs = [
        pl.BlockSpec((tm, D), lambda i, j: (i, 0)),
        pl.BlockSpec((1, D), lambda i, j: (0, 0)),
        pl.BlockSpec((D, bn), lambda i, j: (0, j)),
    ]
    args = [x2d, gain.reshape(1, D).astype(f32), w]
    out_shape = [jax.ShapeDtypeStruct((N // LANE, T, LANE), bf16)]
    out_specs = [pl.BlockSpec((n_groups, tm, LANE), lambda i, j: (j, i, 0))]
    if wf is not None:
        in_specs.append(pl.BlockSpec((D, LANE), lambda i, j: (0, 0)))
        args.append(wf)
        out_shape.append(jax.ShapeDtypeStruct((T, LANE), f32))
        out_specs.append(pl.BlockSpec((tm, LANE), lambda i, j: (i, 0)))
    if n_rot_blocks:
        spb = seq // tm
        for t in rot:
            in_specs.append(pl.BlockSpec((tm, LANE), lambda i, j: (i % spb, 0)))
            args.append(t)
    kern = functools.partial(_in_proj_kernel, n_groups=n_groups,
                             with_forget=wf is not None, n_rot_blocks=n_rot_blocks)
    outs = pl.pallas_call(
        kern,
        grid=grid,
        in_specs=in_specs,
        out_specs=out_specs,
        out_shape=out_shape,
        scratch_shapes=[pltpu.VMEM((tm, D), bf16)],
        compiler_params=pltpu.CompilerParams(
            dimension_semantics=("arbitrary", "arbitrary")),
        name=name,
    )(*args)
    return outs


def _split_bf16(x):
    p0 = x.astype(bf16)
    r1 = x - p0.astype(f32)
    p1 = r1.astype(bf16)
    p2 = (r1 - p1.astype(f32)).astype(bf16)
    return jnp.concatenate([p0, p1, p2], axis=1)


def _cumsum_kernel(f_ref, b_ref, pq_ref, oq_ref, pk_ref, ok_ref, qb_ref, kb_ref, c2_ref,
                   carry_ref, *, tc):
    @pl.when(pl.program_id(1) == 0)
    def _():
        carry_ref[...] = jnp.zeros_like(carry_ref)

    z = f_ref[...] + b_ref[...]
    lf = jnp.minimum(z, 0.0) - jnp.log1p(jnp.exp(-jnp.abs(z)))
    row = lax.broadcasted_iota(jnp.int32, (tc, tc), 0)
    col = lax.broadcasted_iota(jnp.int32, (tc, tc), 1)
    tri = jnp.where(col <= row, 1.0, 0.0).astype(bf16)
    cs = jnp.dot(tri, _split_bf16(lf), preferred_element_type=f32)
    c = (cs[:, :LANE] + cs[:, LANE:2 * LANE] + cs[:, 2 * LANE:]) + carry_ref[...]
    carry_ref[...] = c[tc - 1:tc, :]

    c2 = c * LOG2E
    c2_ref[...] = c2
    pieces = _split_bf16(c2)
    qb = jnp.dot(pieces, pq_ref[...], preferred_element_type=f32) + oq_ref[...]
    kb = jnp.dot(pieces, pk_ref[...], preferred_element_type=f32) + ok_ref[...]
    qb_ref[...] = qb.astype(bf16)
    kb_ref[...] = kb.astype(bf16)


def _bias_placement():
    assert N_HEADS_FOX * BIAS_LANES == LANE and 2 * N_PIECES <= BIAS_LANES
    pq = np.zeros((N_PIECES * LANE, LANE), np.float32)
    pk = np.zeros((N_PIECES * LANE, LANE), np.float32)
    oq = np.zeros((1, LANE), np.float32)
    ok = np.zeros((1, LANE), np.float32)
    for head in range(N_HEADS_FOX):
        base = BIAS_LANES * head
        for piece in range(N_PIECES):
            pq[piece * LANE + head, base + piece] = 1.0
            ok[0, base + piece] = 1.0
            oq[0, base + N_PIECES + piece] = 1.0
            pk[piece * LANE + head, base + N_PIECES + piece] = -1.0
    return (jnp.asarray(pq, bf16), jnp.asarray(oq, f32),
            jnp.asarray(pk, bf16), jnp.asarray(ok, f32))


def _forget_cumsum(f2d, b_pad, *, batch, seq, tc=512):
    tc = min(tc, seq)
    nb = seq // tc
    T = batch * seq
    pq, oq, pk, ok = _bias_placement()
    full = lambda a: pl.BlockSpec(a.shape, lambda b, i: (0, 0))
    row_blk = pl.BlockSpec((tc, LANE), lambda b, i: (b * nb + i, 0))
    return pl.pallas_call(
        functools.partial(_cumsum_kernel, tc=tc),
        grid=(batch, nb),
        in_specs=[row_blk, full(b_pad), full(pq), full(oq), full(pk), full(ok)],
        out_specs=[row_blk, row_blk, row_blk],
        out_shape=[jax.ShapeDtypeStruct((T, LANE), bf16)] * 2
        + [jax.ShapeDtypeStruct((T, LANE), f32)],
        scratch_shapes=[pltpu.VMEM((1, LANE), f32)],
        compiler_params=pltpu.CompilerParams(
            dimension_semantics=("arbitrary", "arbitrary")),
        name="forget_cumsum",
    )(f2d, b_pad, pq, oq, pk, ok)


def _fox_kernel(q_ref, k_ref, v_ref, g_ref, qb_ref, kb_ref, cs_ref, ce_ref, o_ref,
                s_ref, p_ref, alpha_ref, m_ref, acc_ref, j0_ref, *, tq, n_blocks):
    qi = pl.program_id(2)
    lane = lax.broadcasted_iota(jnp.int32, (1, LANE), 1)
    lo = lane < HEAD_DIM
    own = (lo, jnp.logical_not(lo))
    head0 = 2 * pl.program_id(1)
    own_bias = [(lane >= (head0 + h) * BIAS_LANES) & (lane < (head0 + h + 1) * BIAS_LANES)
                for h in range(2)]
    q, qb = q_ref[0, pl.ds(pl.multiple_of(qi * tq, tq), tq), :], qb_ref[...]
    zero = jnp.zeros_like(q)
    q2 = [jnp.concatenate([jnp.where(own[h], q, zero), jnp.where(own_bias[h], qb, zero)],
                          axis=1) for h in range(2)]
    rb = FOX_ROW_BLOCK
    row_in_blk = lax.broadcasted_iota(jnp.int32, (rb, LANE), 0)
    col = lax.broadcasted_iota(jnp.int32, (rb, LANE), 1)

    sel_row = lax.broadcasted_iota(jnp.int32, (2 * LANE, LANE), 0) % LANE
    sel_col = lax.broadcasted_iota(jnp.int32, (2 * LANE, LANE), 1)
    sel = jnp.where(sel_col == (sel_row >= HEAD_DIM).astype(jnp.int32), 1.0, 0.0).astype(bf16)

    def max_sq_norm(x):
        xf = x.astype(f32)
        sq = xf * xf
        hi = sq.astype(bf16)
        lo = (sq - hi.astype(f32)).astype(bf16)
        n2 = jnp.dot(jnp.concatenate([hi, lo], axis=1), sel, preferred_element_type=f32)
        return jnp.max(n2, axis=0, keepdims=True)

    @pl.when(qi == 0)
    def _():
        blocks = [slice(j * tq, (j + 1) * tq) for j in range(n_blocks)]
        k_max = functools.reduce(jnp.maximum, [max_sq_norm(k_ref[0, b, :]) for b in blocks])
        q_max = jnp.concatenate([max_sq_norm(q_ref[0, b, :]) for b in blocks], axis=0)
        qk = 2.0 * jnp.sqrt(q_max * k_max)
        eye = (lax.broadcasted_iota(jnp.int32, (n_blocks, n_blocks), 0)
               == lax.broadcasted_iota(jnp.int32, (n_blocks, n_blocks), 1))
        earlier = (lax.broadcasted_iota(jnp.int32, (n_blocks, n_blocks), 1)
                   < lax.broadcasted_iota(jnp.int32, (n_blocks, n_blocks), 0))
        for h in range(2):
            on_head = lane == head0 + h
            col_of = lambda x, m: jnp.sum(jnp.where(m, x, 0.0), axis=1, keepdims=True)
            hi_i = col_of(qk, lane == h) + col_of(cs_ref[0], on_head)
            ce_j = jnp.sum(jnp.where(eye, col_of(ce_ref[0], on_head), 0.0),
                           axis=0, keepdims=True)
            skip = earlier & (hi_i - ce_j < -FOX_SKIP_LOG2)
            n_skip = jnp.sum(jnp.where(skip, 1.0, 0.0), axis=1, keepdims=True)
            for i in range(n_blocks):
                j0_ref[h, i] = n_skip[i, 0].astype(jnp.int32)

    m_ref[...] = jnp.full(m_ref.shape, NEG, f32)
    acc_ref[...] = jnp.zeros(acc_ref.shape, f32)

    def scores(j, slot, heads):
        r0 = pl.multiple_of(j * tq, tq)
        k2 = jnp.concatenate([k_ref[0, pl.ds(r0, tq), :], kb_ref[pl.ds(r0, tq), :]],
                             axis=1)
        for h in heads:
            s_ref[slot, h] = lax.dot_general(q2[h], k2, (((1,), (1,)), ((), ())),
                                             preferred_element_type=f32)

    def softmax(slot, masked, heads):
        for h in heads:
            for r in range(tq // rb):
                rows = pl.ds(r * rb, rb)
                tiles = []
                for t in range(tq // LANE):
                    s = s_ref[slot, h, rows, t * LANE:(t + 1) * LANE]
                    if masked:
                        s = jnp.where(col + t * LANE <= row_in_blk + r * rb, s, NEG)
                    tiles.append(s)
                mx = functools.reduce(jnp.maximum, tiles)
                m_old = m_ref[h, rows, :]
                m_new = jnp.maximum(m_old, jnp.max(mx, axis=-1, keepdims=True))
                m_ref[h, rows, :] = m_new
                alpha_ref[slot, h, rows, :] = jnp.exp2(m_old - m_new)
                for t, s in enumerate(tiles):
                    p_ref[slot, h, rows, t * LANE:(t + 1) * LANE] = (
                        jnp.exp2(s - m_new).astype(bf16))

    def accumulate(j, slot, heads):
        r0 = pl.multiple_of(j * tq, tq)
        v = v_ref[0, pl.ds(r0, tq), :]
        for h in heads:
            vh = jnp.where(own[h], v, jnp.ones_like(v))
            acc_ref[h] = alpha_ref[slot, h] * acc_ref[h] + jnp.dot(
                p_ref[slot, h], vh, preferred_element_type=f32)

    def step_single(j, masked, heads):
        scores(j, 0, heads)
        softmax(0, masked, heads)
        accumulate(j, 0, heads)
        return 0

    def run_blocks(j_lo, n, heads):
        def step_pair(jj, _):
            j = j_lo + 2 * jj
            scores(j, 0, heads)
            scores(j + 1, 1, heads)
            softmax(0, False, heads)
            accumulate(j, 0, heads)
            softmax(1, False, heads)
            accumulate(j + 1, 1, heads)
            return 0

        lax.fori_loop(0, n // 2, step_pair, 0)
        lax.fori_loop(0, n % 2, lambda _, c: step_single(j_lo + n - 1, False, heads), 0)

    j0 = [j0_ref[h, qi] for h in range(2)]
    j_both = jnp.maximum(j0[0], j0[1])
    run_blocks(j0[0], j_both - j0[0], (0,))
    run_blocks(j0[1], j_both - j0[1], (1,))
    run_blocks(j_both, qi - j_both, (0, 1))
    step_single(qi, True, (0, 1))

    acc0, acc1 = acc_ref[0], acc_ref[1]
    num = jnp.where(lo, acc0, acc1)
    den = pltpu.roll(jnp.where(lo, acc1, acc0), HEAD_DIM, axis=1)
    o_ref[0] = (num / den * _silu(g_ref[0].astype(f32))).astype(bf16)


def _fox_attention(z, qb, kb, c2, *, batch, seq, tq=512):
    tq = min(tq, seq)
    nq = seq // tq
    P = PAIRS_FOX
    T = batch * seq
    c_blocks = c2.reshape(batch, nq, tq, LANE)
    c_first, c_last = c_blocks[:, :, 0, :], c_blocks[:, :, tq - 1, :]
    q_blk = lambda g0: pl.BlockSpec((1, tq, LANE), lambda b, p, i: (g0 + p, b * nq + i, 0))
    kv_blk = lambda g0: pl.BlockSpec((1, seq, LANE), lambda b, p, i: (g0 + p, b, 0))
    c_blk = pl.BlockSpec((1, nq, LANE), lambda b, p, i: (b, 0, 0))
    return pl.pallas_call(
        functools.partial(_fox_kernel, tq=tq, n_blocks=nq),
        grid=(batch, P, nq),
        in_specs=[kv_blk(0), kv_blk(P), kv_blk(2 * P), q_blk(3 * P),
                  pl.BlockSpec((tq, LANE), lambda b, p, i: (b * nq + i, 0)),
                  pl.BlockSpec((seq, LANE), lambda b, p, i: (b, 0)),
                  c_blk, c_blk],
        out_specs=q_blk(0),
        out_shape=jax.ShapeDtypeStruct((P, T, LANE), bf16),
        scratch_shapes=[
            pltpu.VMEM((2, 2, tq, tq), f32),
            pltpu.VMEM((2, 2, tq, tq), bf16),
            pltpu.VMEM((2, 2, tq, LANE), f32),
            pltpu.VMEM((2, tq, LANE), f32),
            pltpu.VMEM((2, tq, LANE), f32),
            pltpu.SMEM((2, nq), jnp.int32),
        ],
        compiler_params=pltpu.CompilerParams(
            dimension_semantics=("arbitrary", "arbitrary", "arbitrary")),
        name="fox_attention",
    )(z, z, z, z, qb, kb, c_first, c_last)


def _dil_kernel(q_ref, kp_ref, kc_ref, vp_ref, vc_ref, g_ref, o_ref,
                qs_ref, ks_ref, vs_ref, q4_ref, k4_ref, v4_ref, os_ref, ls_ref):
    chunk = pl.program_id(2)
    CH, BLK, DEI = DIL_CHUNK, DIL_BLOCK, DIL_DEINTERLEAVE
    qs_ref[...] = q_ref[0].astype(f32)
    ks_ref[0:CH, :] = kp_ref[0].astype(f32)
    ks_ref[CH:2 * CH, :] = kc_ref[0].astype(f32)
    vs_ref[0:CH, :] = vp_ref[0].astype(f32)
    vs_ref[CH:2 * CH, :] = vc_ref[0].astype(f32)
    for res in range(DEI):
        nq, nk = CH // DEI, 2 * CH // DEI
        q4_ref[res * nq:(res + 1) * nq, :] = qs_ref[pl.ds(res, nq, stride=DEI), :]
        k4_ref[res * nk:(res + 1) * nk, :] = ks_ref[pl.ds(res, nk, stride=DEI), :]
        v4_ref[res * nk:(res + 1) * nk, :] = vs_ref[pl.ds(res, nk, stride=DEI), :]

    lane = lax.broadcasted_iota(jnp.int32, (1, LANE), 1)
    lo = lane < HEAD_DIM
    hi = jnp.logical_not(lo)
    qi = lax.broadcasted_iota(jnp.int32, (BLK, 2 * BLK), 0)
    kj = lax.broadcasted_iota(jnp.int32, (BLK, 2 * BLK), 1)
    dist = BLK + qi - kj
    n_tiles = CH // BLK

    for p, (window, r) in enumerate(DIL_PATTERNS):
        n_keys = window // r
        band = (dist >= 0) & (dist <= n_keys)
        bias_band = jnp.where(band, 0.0, NEG)
        bias_first = jnp.where(band & (kj >= BLK), 0.0, NEG)
        span = r * BLK

        def tile(t, r=r, span=span, p=p, bias_band=bias_band, bias_first=bias_first):
            sidx = t // r
            qstart = sidx * span + t % r
            kstart = CH + qstart - span
            no_prev = jnp.logical_and(chunk == 0, sidx == 0)
            bias = jnp.where(no_prev, bias_first, bias_band)
            if r % DEI == 0:
                rr = r // DEI
                res4, sub = (t % r) % DEI, (t % r) // DEI
                q0 = res4 * (CH // DEI) + sidx * (span // DEI) + sub
                k0 = res4 * (2 * CH // DEI) + CH // DEI + sidx * (span // DEI) + sub - span // DEI
                stride = rr if rr > 1 else None
                qt = q4_ref[pl.ds(q0, BLK, stride=stride), :].astype(bf16)
                kt = k4_ref[pl.ds(k0, 2 * BLK, stride=stride), :].astype(bf16)
                vt = v4_ref[pl.ds(k0, 2 * BLK, stride=stride), :].astype(bf16)
            else:
                qt = qs_ref[pl.ds(qstart, BLK, stride=r), :].astype(bf16)
                kt = ks_ref[pl.ds(kstart, 2 * BLK, stride=r), :].astype(bf16)
                vt = vs_ref[pl.ds(kstart, 2 * BLK, stride=r), :].astype(bf16)
            res, ms = [], []
            for own in (lo, hi):
                qh = jnp.where(own, qt, jnp.zeros_like(qt))
                vh = jnp.where(own, vt, jnp.ones_like(vt))
                s = lax.dot_general(qh, kt, (((1,), (1,)), ((), ())),
                                    preferred_element_type=f32) + bias
                m = jnp.max(s, axis=-1, keepdims=True)
                e = jnp.exp2(s - m).astype(bf16)
                res.append(jnp.dot(e, vh, preferred_element_type=f32))
                ms.append(m)
            num = jnp.where(lo, res[0], res[1])
            den = pltpu.roll(jnp.where(lo, res[1], res[0]), HEAD_DIM, axis=1)
            os_ref[p, pl.ds(qstart, BLK, stride=r), :] = num / den
            ls_ref[p, pl.ds(qstart, BLK, stride=r), :] = (
                jnp.where(lo, ms[0], ms[1]) + jnp.log2(den))

        def group(g, _, tile=tile):
            for u in range(DIL_UNROLL):
                tile(g * DIL_UNROLL + u)
            return 0

        lax.fori_loop(0, n_tiles // DIL_UNROLL, group, 0)

    l0, l1, l2 = ls_ref[0], ls_ref[1], ls_ref[2]
    mx = jnp.maximum(jnp.maximum(l0, l1), l2)
    w0, w1, w2 = jnp.exp2(l0 - mx), jnp.exp2(l1 - mx), jnp.exp2(l2 - mx)
    y = (w0 * os_ref[0] + w1 * os_ref[1] + w2 * os_ref[2]) / (w0 + w1 + w2)
    o_ref[0] = (y * _silu(g_ref[0].astype(f32))).astype(bf16)


def _dilated_attention(z, *, batch, seq, group0):
    CH = DIL_CHUNK
    assert seq % CH == 0
    nc = seq // CH
    P = PAIRS_DIL
    T = batch * seq
    g0 = group0

    def cur(off):
        return lambda b, p, c: (g0 + off * P + p, b * nc + c, 0)

    def prev(off):
        return lambda b, p, c: (g0 + off * P + p, b * nc + jnp.maximum(c - 1, 0), 0)

    blk = (1, CH, LANE)
    return pl.pallas_call(
        _dil_kernel,
        grid=(batch, P, nc),
        in_specs=[
            pl.BlockSpec(blk, cur(0)),
            pl.BlockSpec(blk, prev(1)), pl.BlockSpec(blk, cur(1)),
            pl.BlockSpec(blk, prev(2)), pl.BlockSpec(blk, cur(2)),
            pl.BlockSpec(blk, cur(3)),
        ],
        out_specs=pl.BlockSpec(blk, lambda b, p, c: (p, b * nc + c, 0)),
        out_shape=jax.ShapeDtypeStruct((P, T, LANE), bf16),
        scratch_shapes=[
            pltpu.VMEM((CH, LANE), f32),
            pltpu.VMEM((2 * CH, LANE), f32),
            pltpu.VMEM((2 * CH, LANE), f32),
            pltpu.VMEM((CH, LANE), f32),
            pltpu.VMEM((2 * CH, LANE), f32),
            pltpu.VMEM((2 * CH, LANE), f32),
            pltpu.VMEM((len(DIL_PATTERNS), CH, LANE), f32),
            pltpu.VMEM((len(DIL_PATTERNS), CH, LANE), f32),
        ],
        compiler_params=pltpu.CompilerParams(
            dimension_semantics=("arbitrary", "arbitrary", "arbitrary")),
        name="dilated_attention",
    )(z, z, z, z, z, z)


def _ret_kernel(q_ref, k_ref, v_ref, g_ref, intra_ref, qd_ref, kd_ref, cd_ref,
                o_ref, state_ref, *, C, n_chunks):
    @pl.when(pl.program_id(2) == 0)
    def _():
        state_ref[...] = jnp.zeros_like(state_ref)

    nq = RET_QK_DIM // LANE
    nv = RET_V_DIM // LANE
    intra = intra_ref[0]
    qd = qd_ref[0]
    kd = kd_ref[0]
    cd = cd_ref[0]

    st = state_ref[...]
    for ci in range(n_chunks):
        r0 = ci * C
        q = jnp.concatenate([q_ref[c, pl.ds(r0, C), :] for c in range(nq)], axis=1)
        k = jnp.concatenate([k_ref[c, pl.ds(r0, C), :] for c in range(nq)], axis=1)
        v = jnp.concatenate([v_ref[c, pl.ds(r0, C), :] for c in range(nv)], axis=1)
        s = lax.dot_general(q, k, (((1,), (1,)), ((), ())),
                            preferred_element_type=f32) * intra
        inner = jnp.dot(s.astype(bf16), v, preferred_element_type=f32)
        cross = jnp.dot(q, st.astype(bf16), preferred_element_type=f32) * qd
        kdec = (k.astype(f32) * kd).astype(bf16)
        upd = lax.dot_general(kdec, v, (((0,), (0,)), ((), ())),
                              preferred_element_type=f32)
        st = st * cd + upd
        if ci == n_chunks - 1:
            state_ref[...] = st
        y = inner + cross
        mu = jnp.mean(y, axis=-1, keepdims=True)
        yc = y - mu
        var = jnp.mean(yc * yc, axis=-1, keepdims=True)
        yn = yc * lax.rsqrt(var + EPS)
        g = jnp.concatenate([g_ref[c, pl.ds(r0, C), :] for c in range(nv)],
                            axis=1).astype(f32)
        out = (yn * _silu(g)).astype(bf16)
        for c in range(nv):
            o_ref[c, pl.ds(r0, C), :] = out[:, c * LANE:(c + 1) * LANE]


def _retention(z, *, batch, seq, C=RET_CHUNK, tr=1024):
    tr = min(tr, seq)
    nr = seq // tr
    H = N_HEADS_RET
    T = batch * seq
    nq = RET_QK_DIM // LANE
    nv = RET_V_DIM // LANE
    kq0 = H
    v0 = 2 * H * nq // nv
    g0 = v0 + H

    log_gamma = jnp.log1p(-jnp.power(2.0, -5.0 - jnp.arange(H, dtype=f32)))
    pos = jnp.arange(C, dtype=f32)
    rel = pos[:, None] - pos[None, :]
    intra = jnp.where(rel >= 0,
                      jnp.exp(log_gamma[:, None, None] * jnp.maximum(rel, 0.0)), 0.0)
    q_decay = jnp.exp(log_gamma[:, None] * (pos + 1.0))[:, :, None]
    k_decay = jnp.exp(log_gamma[:, None] * (C - 1.0 - pos))[:, :, None]
    chunk_decay = jnp.exp(log_gamma * C)[:, None, None]

    return pl.pallas_call(
        functools.partial(_ret_kernel, C=C, n_chunks=tr // C),
        grid=(batch, H, nr),
        in_specs=[
            pl.BlockSpec((nq, tr, LANE), lambda b, h, i: (h, b * nr + i, 0)),
            pl.BlockSpec((nq, tr, LANE), lambda b, h, i: (kq0 + h, b * nr + i, 0)),
            pl.BlockSpec((nv, tr, LANE), lambda b, h, i: (v0 + h, b * nr + i, 0)),
            pl.BlockSpec((nv, tr, LANE), lambda b, h, i: (g0 + h, b * nr + i, 0)),
            pl.BlockSpec((1, C, C), lambda b, h, i: (h, 0, 0)),
            pl.BlockSpec((1, C, 1), lambda b, h, i: (h, 0, 0)),
            pl.BlockSpec((1, C, 1), lambda b, h, i: (h, 0, 0)),
            pl.BlockSpec((1, 1, 1), lambda b, h, i: (h, 0, 0)),
        ],
        out_specs=pl.BlockSpec((nv, tr, LANE), lambda b, h, i: (h, b * nr + i, 0)),
        out_shape=jax.ShapeDtypeStruct((H * nv, T, LANE), bf16),
        scratch_shapes=[pltpu.VMEM((RET_QK_DIM, RET_V_DIM), f32)],
        compiler_params=pltpu.CompilerParams(
            dimension_semantics=("arbitrary", "arbitrary", "arbitrary")),
        name="retention",
    )(z, z, z, z, intra, q_decay, k_decay, chunk_decay)


def _out_proj_kernel(*refs, n_y, final_norm):
    y_refs = refs[:n_y]
    w_ref, x_ref = refs[n_y], refs[n_y + 1]
    g_ref = refs[n_y + 2] if final_norm else None
    o_ref = refs[-1]
    y = jnp.concatenate([r[c] for r in y_refs for c in range(r.shape[0])], axis=1)
    out = x_ref[...] + jnp.dot(y, w_ref[...], preferred_element_type=f32)
    if final_norm:
        ms = jnp.mean(out * out, axis=-1, keepdims=True)
        out = out * lax.rsqrt(ms + EPS) * g_ref[...]
    o_ref[...] = out


def _out_proj(ys, w, x2d, *, gain=None, tm=1024, name):
    T, D = x2d.shape
    tm = min(tm, T)
    in_specs = [pl.BlockSpec((y.shape[0], tm, LANE), lambda i: (0, i, 0)) for y in ys]
    in_specs += [pl.BlockSpec(w.shape, lambda i: (0, 0)),
                 pl.BlockSpec((tm, D), lambda i: (i, 0))]
    args = list(ys) + [w, x2d]
    if gain is not None:
        in_specs.append(pl.BlockSpec((1, D), lambda i: (0, 0)))
        args.append(gain.reshape(1, D).astype(f32))
    return pl.pallas_call(
        functools.partial(_out_proj_kernel, n_y=len(ys), final_norm=gain is not None),
        grid=(T // tm,),
        in_specs=in_specs,
        out_specs=pl.BlockSpec((tm, D), lambda i: (i, 0)),
        out_shape=jax.ShapeDtypeStruct((T, D), f32),
        compiler_params=pltpu.CompilerParams(dimension_semantics=("arbitrary",)),
        name=name,
    )(*args)


def _even_layer(x2d, g_norm, w_in, b_f, w_out, *, batch, seq):
    D = x2d.shape[1]
    a_w = 4 * WIDTH_FOX
    nf = N_HEADS_FOX
    scale = HEAD_DIM ** -0.5
    col_scale = jnp.ones((a_w + 4 * WIDTH_DIL,), f32)
    col_scale = col_scale.at[:WIDTH_FOX].set(scale * LOG2E)
    col_scale = col_scale.at[a_w:a_w + WIDTH_DIL].set(scale * LOG2E)
    w_main = jnp.concatenate([w_in[:, :a_w], w_in[:, a_w + nf:]], axis=1) * col_scale
    w_f = jnp.pad(w_in[:, a_w:a_w + nf], ((0, 0), (0, LANE - nf)))
    z, f2d = _in_proj(x2d, g_norm, w_main.astype(bf16), seq=seq, wf=w_f.astype(bf16),
                      name="even_in_proj")
    b_pad = jnp.pad(b_f.astype(f32), (0, LANE - nf)).reshape(1, LANE)
    qb, kb, c2 = _forget_cumsum(f2d, b_pad, batch=batch, seq=seq)
    ya = _fox_attention(z, qb, kb, c2, batch=batch, seq=seq)
    yb = _dilated_attention(z, batch=batch, seq=seq, group0=4 * PAIRS_FOX)
    return _out_proj([ya, yb], w_out.astype(bf16), x2d, name="even_out_proj")


def _odd_layer(x2d, g_norm, w_in, w_out, *, batch, seq, final_gain):
    H, dk = N_HEADS_RET, RET_QK_DIM
    qk_w = H * dk
    half = dk // 2
    perm = (jnp.arange(H)[:, None, None] * dk
            + jnp.arange(2)[None, :, None]
            + 2 * jnp.arange(half)[None, None, :]).reshape(-1)
    w_q = w_in[:, :qk_w][:, perm]
    w_k = w_in[:, qk_w:2 * qk_w][:, perm] * (dk ** -0.5)
    w_all = jnp.concatenate([w_q, w_k, w_in[:, 2 * qk_w:]], axis=1).astype(bf16)

    inv = 1.0 / (ROT_BASE ** jnp.linspace(0.0, 1.0, half, dtype=f32))
    ang = jnp.arange(seq, dtype=f32)[:, None] * inv[None, :]
    rot = (jnp.cos(ang), jnp.sin(ang))
    (z,) = _in_proj(x2d, g_norm, w_all, seq=seq, rot=rot, n_rot_blocks=2 * qk_w // IN_PROJ_BN,
                    name="odd_in_proj")
    yc = _retention(z, batch=batch, seq=seq)
    return _out_proj([yc], w_out.astype(bf16), x2d, gain=final_gain, name="odd_out_proj")


def kernel(x, even_norm, even_w_in, even_b_f, even_w_out, odd_norm, odd_w_in, odd_w_out, final_norm):
    batch, seq, D = x.shape
    x2d = x.reshape(batch * seq, D)
    x2d = _even_layer(x2d, even_norm[0], even_w_in[0], even_b_f[0], even_w_out[0],
                      batch=batch, seq=seq)
    out = _odd_layer(x2d, odd_norm[0], odd_w_in[0], odd_w_out[0],
                     batch=batch, seq=seq, final_gain=final_norm)
    return out.reshape(batch, seq, D)
```
